```python
import math
import jax
import jax.numpy as jnp
from jax import lax
import numpy as np

D_MODEL = 1024
BATCH = 8
SEQ = 2048
DEPTH = 4

CTX_LEN = 256
GRID_W = 64
D_A = 512
H_A = 8
HD_A = D_A // H_A
H_B = 4
DK_B = 128
DV_B = 128
D_B = H_B * DV_B
H_C = 4
DK_C = 128
DV_C = 128
D_C = H_C * DV_C
D_MIX = D_A + D_B + D_C
CONV_W = 4
CONV_PAD = 2
CHUNK_B = 64
CHUNK_C = 16
RG_C = 8.0
EPS = 1e-6
SPLIT_SIZES = (D_A, D_A, 2 * H_B * DK_B + D_B, 2 * H_B, 2 * H_B, D_B,
               H_C * DK_C, 2 * H_C * DK_C, H_C * DV_C, D_C)
N_IN = sum(SPLIT_SIZES)

kernel_name = 'hybrid_rglru_gdn_hgrn2_prefix_dit'


def rms_norm(x, w):
    xf = x.astype(jnp.float32)
    y = xf * lax.rsqrt(jnp.mean(xf * xf, axis=-1, keepdims=True) + EPS)
    return (y * w.astype(jnp.float32)).astype(x.dtype)


def l2norm(x):
    xf = x.astype(jnp.float32)
    return xf * lax.rsqrt(jnp.sum(xf * xf, axis=-1, keepdims=True) + EPS)


def _rev(a):
    return jnp.flip(a, axis=1)


def dw_conv(x, w):
    return lax.conv_general_dilated(
        x, w[:, None, :].astype(x.dtype), window_strides=(1,),
        padding=[(CONV_PAD, CONV_W - 1 - CONV_PAD)],
        dimension_numbers=('NWC', 'WIO', 'NWC'), feature_group_count=x.shape[-1])


def _to_chunks(x, chunk):
    bsz, t_len = x.shape[:2]
    x = x.reshape((bsz, t_len // chunk, chunk) + x.shape[2:])
    return jnp.moveaxis(x, 3, 2)


def _from_chunks(x):
    x = jnp.moveaxis(x, 2, 3)
    return x.reshape((x.shape[0], x.shape[1] * x.shape[2]) + x.shape[3:])


def to_col_major(u, rows):
    bsz, t_len, d = u.shape
    return u.reshape(bsz, rows, GRID_W, d).transpose(0, 2, 1, 3).reshape(bsz, t_len, d)


def from_col_major(u, rows):
    bsz, t_len, d = u.shape
    return u.reshape(bsz, GRID_W, rows, d).transpose(0, 2, 1, 3).reshape(bsz, t_len, d)


def rglru(x, w_r, b_r, w_i, b_i, lam, h0):
    bsz, t_len, _ = x.shape
    xf = x.astype(jnp.float32)
    xh = xf.reshape(bsz, t_len, H_A, HD_A)
    r = jax.nn.sigmoid(jnp.einsum('bthi,hij->bthj', xh, w_r.astype(jnp.float32)).reshape(bsz, t_len, D_A) + b_r)
    i = jax.nn.sigmoid(jnp.einsum('bthi,hij->bthj', xh, w_i.astype(jnp.float32)).reshape(bsz, t_len, D_A) + b_i)
    log_a = -RG_C * r * jax.nn.softplus(-lam.astype(jnp.float32))
    a = jnp.exp(log_a)
    u = jnp.sqrt(-jnp.expm1(2.0 * log_a)) * (i * xf)

    def combine(lhs, rhs):
        a1, u1 = lhs
        a2, u2 = rhs
        return a1 * a2, a2 * u1 + u2

    a_cum, h = lax.associative_scan(combine, (a, u), axis=1)
    h = h + a_cum * h0[:, None, :]
    return h, h[:, -1]


def gated_delta_chunked(q, k, v, g, beta, s0):
    f32 = jnp.float32
    q = _to_chunks(q.astype(f32) * DK_B ** -0.5, CHUNK_B)
    k = _to_chunks(k.astype(f32), CHUNK_B)
    v = _to_chunks(v.astype(f32), CHUNK_B)
    g = jnp.cumsum(_to_chunks(g.astype(f32), CHUNK_B), axis=-1)
    beta = _to_chunks(beta.astype(f32), CHUNK_B)
    idx = jnp.arange(CHUNK_B)
    causal = idx[:, None] >= idx[None, :]
    strict = idx[:, None] > idx[None, :]
    decay = jnp.exp(jnp.where(causal, g[..., :, None] - g[..., None, :], -jnp.inf))
    kb = k * beta[..., None]
    a_int = jnp.where(strict, jnp.einsum('bnhcd,bnhsd->bnhcs', kb, k) * decay, 0.0)
    eye = jnp.eye(CHUNK_B, dtype=f32)
    t_inv = lax.linalg.triangular_solve(eye + a_int, jnp.broadcast_to(eye, a_int.shape),
                                        left_side=True, lower=True, unit_diagonal=True)
    u = jnp.einsum('bnhcs,bnhse->bnhce', t_inv, v * beta[..., None])
    w = jnp.einsum('bnhcs,bnhsd->bnhcd', t_inv, kb * jnp.exp(g)[..., None])
    qk = jnp.einsum('bnhcd,bnhsd->bnhcs', q, k) * decay
    q_dec = q * jnp.exp(g)[..., None]
    g_last = g[..., -1]
    k_dec = k * jnp.exp(g_last[..., None] - g)[..., None]

    def step(s, inp):
        qk_n, u_n, w_n, q_n, k_n, gl_n = inp
        v_new = u_n - jnp.einsum('bhcd,bhde->bhce', w_n, s)
        o = jnp.einsum('bhcd,bhde->bhce', q_n, s) + jnp.einsum('bhcs,bhse->bhce', qk_n, v_new)
        s = s * jnp.exp(gl_n)[..., None, None] + jnp.einsum('bhcd,bhce->bhde', k_n, v_new)
        return s, o

    xs = tuple(jnp.moveaxis(t, 1, 0) for t in (qk, u, w, q_dec, k_dec, g_last))
    s_fin, o = lax.scan(step, s0, xs)
    return _from_chunks(jnp.moveaxis(o, 0, 1)), s_fin


def gla_chunked(q, k, v, log_f, s0):
    f32 = jnp.float32
    q = _to_chunks(q.astype(f32), CHUNK_C)
    k = _to_chunks(k.astype(f32), CHUNK_C)
    v = _to_chunks(v.astype(f32), CHUNK_C)
    b = jnp.cumsum(_to_chunks(log_f.astype(f32), CHUNK_C), axis=-2)
    idx = jnp.arange(CHUNK_C)
    causal = idx[:, None] >= idx[None, :]
    dec = jnp.exp(jnp.where(causal[:, :, None], b[..., :, None, :] - b[..., None, :, :], -jnp.inf))
    scores = jnp.einsum('bnhcd,bnhsd,bnhcsd->bnhcs', q, k, dec)
    o_intra = jnp.einsum('bnhcs,bnhse->bnhce', scores, v)
    b_last = b[..., -1, :]
    q_dec = q * jnp.exp(b)
    k_dec = k * jnp.exp(b_last[..., None, :] - b)
    upd = jnp.einsum('bnhcd,bnhce->bnhde', k_dec, v)

    def step(s, inp):
        dec_n, upd_n = inp
        return s * dec_n[..., None] + upd_n, s

    s_fin, s_start = lax.scan(step, s0, (jnp.moveaxis(jnp.exp(b_last), 1, 0), jnp.moveaxis(upd, 1, 0)))
    o_inter = jnp.einsum('bnhcd,nbhde->bnhce', q_dec, s_start)
    return _from_chunks(o_intra + o_inter), s_fin


def mixer_core(u, w_in_l, conv_a_w_l, conv_a_b_l, rg_w_r_l, rg_b_r_l, rg_w_i_l, rg_b_i_l, rg_lam_l,
               conv_b_w_l, a_log_l, dt_bias_l, lb_l, init):
    bsz, t_len, _ = u.shape
    f32 = jnp.float32
    z = jnp.einsum('btd,dn->btn', u, w_in_l)
    xa, ga, qkv, a_lg, b_lg, gb, qc, fc, ic, gc = jnp.split(z, np.cumsum(SPLIT_SIZES)[:-1].tolist(), axis=-1)
    a_f0, a_b0, b_f0, b_b0, c_f0, c_b0 = init
    xa = dw_conv(xa, conv_a_w_l) + conv_a_b_l
    ha_f, a_f1 = rglru(xa, rg_w_r_l[0], rg_b_r_l[0], rg_w_i_l[0], rg_b_i_l[0], rg_lam_l[0], a_f0)
    ha_b, a_b1 = rglru(_rev(xa), rg_w_r_l[1], rg_b_r_l[1], rg_w_i_l[1], rg_b_i_l[1], rg_lam_l[1], a_b0)
    ya = ha_f + _rev(ha_b)
    qkv = jax.nn.silu(dw_conv(qkv, conv_b_w_l))
    qb, kb, vb = jnp.split(qkv, [H_B * DK_B, 2 * H_B * DK_B], axis=-1)
    qb = l2norm(qb.reshape(bsz, t_len, H_B, DK_B))
    kb = l2norm(kb.reshape(bsz, t_len, H_B, DK_B))
    vb = vb.reshape(bsz, t_len, H_B, DV_B)
    g = -jnp.exp(a_log_l) * jax.nn.softplus(a_lg.reshape(bsz, t_len, 2, H_B).astype(f32) + dt_bias_l)
    beta = jax.nn.sigmoid(b_lg.reshape(bsz, t_len, 2, H_B).astype(f32))
    ob_f, b_f1 = gated_delta_chunked(qb, kb, vb, g[:, :, 0], beta[:, :, 0], b_f0)
    ob_b, b_b1 = gated_delta_chunked(_rev(qb), _rev(kb), _rev(vb), _rev(g[:, :, 1]), _rev(beta[:, :, 1]), b_b0)
    ob = ob_f + _rev(ob_b)
    qc = jax.nn.silu(qc).reshape(bsz, t_len, H_C, DK_C)
    ic = ic.reshape(bsz, t_len, H_C, DV_C)
    fg = lb_l + (1.0 - lb_l) * jax.nn.sigmoid(fc.reshape(bsz, t_len, 2, H_C * DK_C).astype(f32))
    kc = (1.0 - fg).reshape(bsz, t_len, 2, H_C, DK_C)
    log_f = jnp.log(fg).reshape(bsz, t_len, 2, H_C, DK_C)
    oc_f, c_f1 = gla_chunked(qc, kc[:, :, 0], ic, log_f[:, :, 0], c_f0)
    oc_b, c_b1 = gla_chunked(_rev(qc), _rev(kc[:, :, 1]), _rev(ic), _rev(log_f[:, :, 1]), c_b0)
    oc = oc_f + _rev(oc_b)
    dt = u.dtype
    core = (ya.astype(dt), ga, ob.astype(dt), gb, oc.astype(dt), gc)
    states = (a_f1, a_b1, b_f1, b_b1, c_f1, c_b1)
    return core, states


def merge(core, gdn_norm_l, hg_norm_l):
    ya, ga, ob, gb, oc, gc = core
    bsz, t_len = ya.shape[:2]
    yb = rms_norm(ob, gdn_norm_l).reshape(bsz, t_len, D_B)
    yc = rms_norm(oc, hg_norm_l).reshape(bsz, t_len, D_C)
    return jnp.concatenate([ya * jax.nn.silu(ga), yb * jax.nn.silu(gb), yc * jax.nn.silu(gc)], axis=-1)


def setup_inputs(seed: int = 0) -> dict:
    key = jax.random.key(seed)
    ks = jax.random.split(key, 24)
    f32 = jnp.float32

    def nrm(k, shape, s):
        return jax.random.normal(k, shape, f32) * s

    a8 = jax.random.uniform(ks[11], (DEPTH, 2, D_A), f32, 0.9, 0.999)
    sig = a8 ** (1.0 / RG_C)
    rg_lam = jnp.log(sig) - jnp.log1p(-sig)
    a_log = jnp.log(jax.random.uniform(ks[13], (DEPTH, 2, H_B), f32, 1.0, 16.0))
    dt = jnp.exp(jax.random.uniform(ks[14], (DEPTH, 2, H_B), f32, math.log(1e-3), math.log(1e-1)))
    dt_bias = dt + jnp.log(-jnp.expm1(-dt))
    return {
        'x': nrm(ks[0], (BATCH, SEQ, D_MODEL), 1.0),
        'c': nrm(ks[1], (BATCH, D_MODEL), 1.0),
        'ctx': nrm(ks[2], (BATCH, CTX_LEN, D_MODEL), 1.0),
        'c_ctx': nrm(ks[3], (D_MODEL,), 1.0),
        'w_ada': nrm(ks[4], (DEPTH, D_MODEL, 3 * D_MODEL), 0.5 * D_MODEL ** -0.5),
        'b_ada': nrm(ks[5], (DEPTH, 3 * D_MODEL), 0.02),
        'norm_pre': 1.0 + nrm(ks[6], (DEPTH, D_MODEL), 0.02),
        'norm_post': 1.0 + nrm(ks[7], (DEPTH, D_MODEL), 0.02),
        'w_in': nrm(ks[8], (DEPTH, D_MODEL, N_IN), D_MODEL ** -0.5),
        'conv_a_w': nrm(ks[9], (DEPTH, CONV_W, D_A), CONV_W ** -0.5),
        'conv_a_b': nrm(ks[10], (DEPTH, D_A), 0.02),
        'rg_w_r': nrm(ks[12], (DEPTH, 2, H_A, HD_A, HD_A), HD_A ** -0.5),
        'rg_b_r': nrm(ks[15], (DEPTH, 2, D_A), 0.1),
        'rg_w_i': nrm(ks[16], (DEPTH, 2, H_A, HD_A, HD_A), HD_A ** -0.5),
        'rg_b_i': nrm(ks[17], (DEPTH, 2, D_A), 0.1),
        'rg_lam': rg_lam,
        'conv_b_w': nrm(ks[18], (DEPTH, CONV_W, 2 * H_B * DK_B + D_B), CONV_W ** -0.5),
        'gdn_a_log': a_log,
        'gdn_dt_bias': dt_bias,
        'gdn_norm': 1.0 + nrm(ks[19], (DEPTH, DV_B), 0.02),
        'hg_lb': nrm(ks[20], (DEPTH, 2, H_C * DK_C), 0.1),
        'hg_norm': 1.0 + nrm(ks[21], (DEPTH, DV_C), 0.02),
        'w_out': nrm(ks[22], (DEPTH, D_MIX, D_MODEL), D_MIX ** -0.5),
    }


def reference(x, c, ctx, c_ctx, w_ada, b_ada, norm_pre, norm_post, w_in, conv_a_w, conv_a_b,
              rg_w_r, rg_b_r, rg_w_i, rg_b_i, rg_lam, conv_b_w, gdn_a_log, gdn_dt_bias, gdn_norm,
              hg_lb, hg_norm, w_out):
    f32 = jnp.float32
    bsz, t_len, _ = x.shape
    rows = t_len // GRID_W
    lb_w = jax.nn.softmax(hg_lb.astype(f32), axis=0)
    lbs = jnp.cumsum(lb_w, axis=0) - lb_w[0]
    zero_states = (jnp.zeros((bsz, D_A), f32), jnp.zeros((bsz, D_A), f32),
                   jnp.zeros((bsz, H_B, DK_B, DV_B), f32), jnp.zeros((bsz, H_B, DK_B, DV_B), f32),
                   jnp.zeros((bsz, H_C, DK_C, DV_C), f32), jnp.zeros((bsz, H_C, DK_C, DV_C), f32))
    sc = jax.nn.silu(c)
    scc = jax.nn.silu(c_ctx)
    h, hc = x, ctx
    for l in range(DEPTH):
        shift_x, scale_x, gate_x = jnp.split(jnp.einsum('bd,de->be', sc, w_ada[l]) + b_ada[l], 3, axis=-1)
        shift_c, scale_c, gate_c = jnp.split(jnp.einsum('d,de->e', scc, w_ada[l]) + b_ada[l], 3, axis=-1)
        layer_w = (w_in[l], conv_a_w[l], conv_a_b[l], rg_w_r[l], rg_b_r[l], rg_w_i[l], rg_b_i[l], rg_lam[l],
                   conv_b_w[l], gdn_a_log[l], gdn_dt_bias[l], lbs[l])
        uc = rms_norm(hc, norm_pre[l]) * (1.0 + scale_c) + shift_c
        core_c, ctx_states = mixer_core(uc, *layer_w, zero_states)
        ux = rms_norm(h, norm_pre[l]) * (1.0 + scale_x[:, None]) + shift_x[:, None]
        transposed = (l % 2 == 1)
        if transposed:
            ux = to_col_major(ux, rows)
        core_x, _ = mixer_core(ux, *layer_w, ctx_states)
        yx = merge(core_x, gdn_norm[l], hg_norm[l])
        if transposed:
            yx = from_col_major(yx, rows)
        h = h + gate_x[:, None] * rms_norm(jnp.einsum('btm,md->btd', yx, w_out[l]), norm_post[l])
        if l < DEPTH - 1:
            yc = merge(core_c, gdn_norm[l], hg_norm[l])
            hc = hc + gate_c * rms_norm(jnp.einsum('btm,md->btd', yc, w_out[l]), norm_post[l])
    return h
```

```python
import functools

import numpy as np
import jax
import jax.numpy as jnp
from jax import lax
from jax.experimental import pallas as pl
from jax.experimental.pallas import tpu as pltpu

F32 = jnp.float32
BF16 = jnp.bfloat16
HI = lax.Precision.HIGHEST

GRID_W = 64
H_A, HD_A = 8, 64
H_B, DK_B = 4, 128
H_C, DK_C = 4, 128
D_HEADS = 512
CONV_W = 4
CONV_PAD = 2
RG_C = 8.0
EPS = 1e-6

LANES = 128
SUBLANES = 8
ROW_TILE = 256
CHUNK_B = 64
TILE_C = 128
VMEM_LIMIT = 56 * 1024 * 1024

COL_XA, COL_GA, COL_Q, COL_K, COL_V, COL_GB = 0, 4, 8, 12, 16, 20
COL_QC, COL_FF, COL_FB, COL_IC, COL_GC, COL_AB = 24, 28, 32, 36, 40, 44
N_PROJ = 45 * LANES


def _dot(a, b):
    return jnp.dot(a.astype(BF16), b.astype(BF16), preferred_element_type=F32)


def _dot_nt(a, b):
    return lax.dot_general(a.astype(BF16), b.astype(BF16), (((1,), (1,)), ((), ())),
                           preferred_element_type=F32)


def _dot_tn(a, b):
    return lax.dot_general(a.astype(BF16), b.astype(BF16), (((0,), (0,)), ((), ())),
                           preferred_element_type=F32)


def _dot_hi(a, b):
    return jnp.dot(a, b, precision=HI, preferred_element_type=F32)


def _split3_lhs(x):
    hi = x.astype(BF16)
    lo = (x - hi.astype(F32)).astype(BF16)
    return jnp.concatenate([hi, hi, lo], axis=1)


def _split3_rhs(x):
    hi = x.astype(BF16)
    lo = (x - hi.astype(F32)).astype(BF16)
    return jnp.concatenate([hi, lo, hi], axis=0)


def _dot3(lhs3, rhs3):
    return jnp.dot(lhs3, rhs3, preferred_element_type=F32)


def _sigmoid(x):
    return 1.0 / (1.0 + jnp.exp(-x))


def _silu(x):
    return x * _sigmoid(x)


def _softplus(x):
    return jnp.maximum(x, 0.0) + jnp.log1p(jnp.exp(-jnp.abs(x)))


def _rms(x):
    return x * lax.rsqrt(jnp.mean(x * x, axis=-1, keepdims=True) + EPS)


def _params(**kw):
    return pltpu.CompilerParams(vmem_limit_bytes=VMEM_LIMIT, **kw)


def _ada_kernel(s_ref, w_ref, b_ref, o_ref):
    o_ref[...] = _dot_hi(_silu(s_ref[...]), w_ref[...]) + b_ref[...]


def _ada(cs, w_ada, b_ada):
    depth, d, n3 = w_ada.shape
    tn = 1024
    return pl.pallas_call(
        _ada_kernel,
        grid=(depth, n3 // tn),
        in_specs=[pl.BlockSpec((16, d), lambda l, j: (0, 0)),
                  pl.BlockSpec((None, d, tn), lambda l, j: (l, 0, j)),
                  pl.BlockSpec((None, 1, tn), lambda l, j: (l, 0, j))],
        out_specs=pl.BlockSpec((None, 16, tn), lambda l, j: (l, 0, j)),
        out_shape=jax.ShapeDtypeStruct((depth, 16, n3), F32),
        compiler_params=_params(),
        name="ada",
    )(cs, w_ada, b_ada.reshape(depth, 1, n3))


def _inproj_kernel(h_ref, sh_ref, sc_ref, npre_ref, w_ref, z_ref, *, colmajor, cn):
    d = npre_ref.shape[-1]
    if colmajor:
        x = jnp.concatenate([h_ref[:, j * d:(j + 1) * d] for j in range(h_ref.shape[-1] // d)], axis=0)
    else:
        x = h_ref[...]
    u = (_rms(x) * npre_ref[...] * (1.0 + sc_ref[...]) + sh_ref[...]).astype(BF16)
    for j in range(z_ref.shape[-1] // cn):
        z_ref[:, j * cn:(j + 1) * cn] = jnp.dot(u, w_ref[:, j * cn:(j + 1) * cn],
                                               preferred_element_type=F32)


def _inproj(h, shift, scale, npre, w, *, colmajor):
    bsz, t_len, d = h.shape
    n = w.shape[-1]
    tc = ROW_TILE
    if colmajor:
        rows = t_len // GRID_W
        h_in = h.reshape(bsz, rows, GRID_W * d)
        h_spec = pl.BlockSpec((None, rows, (tc // rows) * d), lambda b, i: (b, 0, i))
    else:
        h_in = h
        h_spec = pl.BlockSpec((None, tc, d), lambda b, i: (b, i, 0))
    vec = pl.BlockSpec((None, 1, d), lambda b, i: (b, 0, 0))
    return pl.pallas_call(
        functools.partial(_inproj_kernel, colmajor=colmajor, cn=1152),
        grid=(bsz, t_len // tc),
        in_specs=[h_spec, vec, vec,
                  pl.BlockSpec((1, d), lambda b, i: (0, 0)),
                  pl.BlockSpec((d, n), lambda b, i: (0, 0))],
        out_specs=pl.BlockSpec((None, tc, n), lambda b, i: (b, i, 0)),
        out_shape=jax.ShapeDtypeStruct((bsz, t_len, n), F32),
        compiler_params=_params(),
        name="inproj",
    )(h_in, shift, scale, npre, w)


def _outproj_kernel(ya_ref, yb_ref, yc_ref, w_ref, h_ref, gt_ref, npost_ref, o_ref, *, colmajor):
    dm = ya_ref.shape[-1]
    d = npost_ref.shape[-1]
    o = (jnp.dot(ya_ref[...].astype(BF16), w_ref[0:dm, :], preferred_element_type=F32)
         + jnp.dot(yb_ref[...].astype(BF16), w_ref[dm:2 * dm, :], preferred_element_type=F32)
         + jnp.dot(yc_ref[...].astype(BF16), w_ref[2 * dm:3 * dm, :], preferred_element_type=F32))
    upd = gt_ref[...] * (_rms(o) * npost_ref[...])
    if colmajor:
        rows = h_ref.shape[0]
        for j in range(h_ref.shape[-1] // d):
            o_ref[:, j * d:(j + 1) * d] = h_ref[:, j * d:(j + 1) * d] + upd[j * rows:(j + 1) * rows, :]
    else:
        o_ref[...] = h_ref[...] + upd


def _outproj(ya, yb, yc, w, h, gate, npost, *, colmajor):
    bsz, t_len, d = h.shape
    dm = ya.shape[-1]
    tc = ROW_TILE
    if colmajor:
        rows = t_len // GRID_W
        h_in = h.reshape(bsz, rows, GRID_W * d)
        h_spec = pl.BlockSpec((None, rows, (tc // rows) * d), lambda b, i: (b, 0, i))
    else:
        h_in = h
        h_spec = pl.BlockSpec((None, tc, d), lambda b, i: (b, i, 0))
    y_spec = pl.BlockSpec((None, tc, dm), lambda b, i: (b, i, 0))
    out = pl.pallas_call(
        functools.partial(_outproj_kernel, colmajor=colmajor),
        grid=(bsz, t_len // tc),
        in_specs=[y_spec, y_spec, y_spec,
                  pl.BlockSpec((3 * dm, d), lambda b, i: (0, 0)),
                  h_spec,
                  pl.BlockSpec((None, 1, d), lambda b, i: (b, 0, 0)),
                  pl.BlockSpec((1, d), lambda b, i: (0, 0))],
        out_specs=h_spec,
        out_shape=jax.ShapeDtypeStruct(h_in.shape, F32),
        compiler_params=_params(),
        name="outproj",
    )(ya, yb, yc, w, h_in, gate, npost)
    return out.reshape(bsz, t_len, d)


def _conv_tile(pad_ref, cw_ref, i):
    base = i * ROW_TILE + SUBLANES - CONV_PAD
    acc = cw_ref[0:1, :] * pad_ref[base:base + ROW_TILE, :]
    for j in range(1, CONV_W):
        acc = acc + cw_ref[j:j + 1, :] * pad_ref[base + j:base + j + ROW_TILE, :]
    return acc


def _fill_pad(pad_ref, x_ref, t_len):
    zeros = jnp.zeros((SUBLANES, pad_ref.shape[-1]), F32)
    pad_ref[0:SUBLANES, :] = zeros
    pad_ref[SUBLANES:SUBLANES + t_len, :] = x_ref[...]
    pad_ref[SUBLANES + t_len:2 * SUBLANES + t_len, :] = zeros


def _rglru_kernel(xc_ref, gc_ref, xx_ref, gx_ref, cw_ref, cb_ref, wg_ref, br_ref, bi_ref, lam_ref,
                  oc_ref, ox_ref, pad_ref, a_ref, u_ref):
    dh = xc_ref.shape[-1]
    half = dh // 2
    n_grp = ROW_TILE // SUBLANES
    sp = _softplus(-lam_ref[...])
    row = lax.broadcasted_iota(jnp.int32, (ROW_TILE, dh), 0) % SUBLANES

    def gates(i, dirn):
        xc = _conv_tile(pad_ref, cw_ref, i) + cb_ref[...]
        pre = [_dot(xc[:, hf * half:(hf + 1) * half], wg_ref[hf, :, dirn * dh:(dirn + 1) * dh])
               for hf in range(2)]
        r_pre = jnp.concatenate([pre[0][:, :half], pre[1][:, :half]], axis=1)
        i_pre = jnp.concatenate([pre[0][:, half:], pre[1][:, half:]], axis=1)
        r = _sigmoid(r_pre + br_ref[dirn:dirn + 1, :])
        ig = _sigmoid(i_pre + bi_ref[dirn:dirn + 1, :])
        log_a = -RG_C * r * sp[dirn:dirn + 1, :]
        a = jnp.exp(log_a)
        u = jnp.sqrt(-jnp.tanh(log_a) * (a * a + 1.0)) * (ig * xc)
        return a, u

    def seg(x_ref, g_ref, o_ref, t_len, h0):
        n_tiles = t_len // ROW_TILE
        _fill_pad(pad_ref, x_ref, t_len)
        carry = h0[0]
        for i in range(n_tiles):
            a, u = gates(i, 0)
            for sft in (1, 2, 4):
                keep = row >= sft
                u = u + a * jnp.where(keep, pltpu.roll(u, sft, 0), 0.0)
                a = a * jnp.where(keep, pltpu.roll(a, sft, 0), 1.0)
            a_ref[...] = a
            u_ref[...] = u

            def fbody(g, cr, i=i):
                r0 = pl.multiple_of(g * SUBLANES, SUBLANES)
                hh = u_ref[pl.ds(r0, SUBLANES), :] + a_ref[pl.ds(r0, SUBLANES), :] * cr
                o_ref[pl.ds(i * ROW_TILE + r0, SUBLANES), :] = hh
                return hh[SUBLANES - 1:SUBLANES, :]

            carry = lax.fori_loop(0, n_grp, fbody, carry)
        hf_end = carry
        carry = h0[1]
        for i in reversed(range(n_tiles)):
            a, u = gates(i, 1)
            for sft in (1, 2, 4):
                keep = row < SUBLANES - sft
                u = u + a * jnp.where(keep, pltpu.roll(u, ROW_TILE - sft, 0), 0.0)
                a = a * jnp.where(keep, pltpu.roll(a, ROW_TILE - sft, 0), 1.0)
            a_ref[...] = a
            u_ref[...] = u

            def bbody(gi, cr, i=i):
                r0 = pl.multiple_of((n_grp - 1 - gi) * SUBLANES, SUBLANES)
                hh = u_ref[pl.ds(r0, SUBLANES), :] + a_ref[pl.ds(r0, SUBLANES), :] * cr
                rows = pl.ds(i * ROW_TILE + r0, SUBLANES)
                o_ref[rows, :] = (o_ref[rows, :] + hh) * _silu(g_ref[rows, :])
                return hh[0:1, :]

            carry = lax.fori_loop(0, n_grp, bbody, carry)
        return hf_end, carry

    zero = jnp.zeros((1, dh), F32)
    states = seg(xc_ref, gc_ref, oc_ref, xc_ref.shape[0], (zero, zero))
    seg(xx_ref, gx_ref, ox_ref, xx_ref.shape[0], states)


def _rglru(zc, zx, cw, cb, wg, br, bi, lam):
    bsz, tc_len, _ = zc.shape
    tx_len = zx.shape[1]
    dh = D_HEADS
    nb = dh // LANES

    def zspec(t_len, col):
        return pl.BlockSpec((None, t_len, dh), lambda b: (b, 0, col // nb))

    def full(a):
        return pl.BlockSpec(a.shape, lambda b: (0,) * a.ndim)

    return pl.pallas_call(
        _rglru_kernel,
        grid=(bsz,),
        in_specs=[zspec(tc_len, COL_XA), zspec(tc_len, COL_GA), zspec(tx_len, COL_XA), zspec(tx_len, COL_GA),
                  full(cw), full(cb), full(wg), full(br), full(bi), full(lam)],
        out_specs=[pl.BlockSpec((None, tc_len, dh), lambda b: (b, 0, 0)),
                   pl.BlockSpec((None, tx_len, dh), lambda b: (b, 0, 0))],
        out_shape=[jax.ShapeDtypeStruct((bsz, tc_len, dh), F32),
                   jax.ShapeDtypeStruct((bsz, tx_len, dh), F32)],
        scratch_shapes=[pltpu.VMEM((tx_len + 2 * SUBLANES, dh), F32),
                        pltpu.VMEM((ROW_TILE, dh), F32),
                        pltpu.VMEM((ROW_TILE, dh), F32)],
        compiler_params=_params(),
        name="rglru",
    )(zc, zc, zx, zx, cw, cb, wg, br, bi, lam)


def _gdn_kernel(qc_ref, kc_ref, vc_ref, abc_ref, gtc_ref, qx_ref, kx_ref, vx_ref, abx_ref, gtx_ref,
                cwq_ref, cwk_ref, cwv_ref, prm_ref, nw_ref, oc_ref, ox_ref,
                pad_ref, q_s, k_s, v_s, gf_s, bf_s, gb_s, bb_s, of_s):
    head = pl.program_id(1)
    c = CHUNK_B
    ii = lax.broadcasted_iota(jnp.int32, (c, c), 0)
    jj = lax.broadcasted_iota(jnp.int32, (c, c), 1)
    eye = (ii == jj).astype(F32)
    ones = jnp.ones((c, c), F32)
    low, upp, slow, supp = ii >= jj, ii <= jj, ii > jj, ii < jj
    incl = (low.astype(F32), upp.astype(F32))
    strk = (slow.astype(F32), supp.astype(F32))
    a_mask = (slow, supp)
    s_mask = (low, upp)
    mst = (jnp.concatenate([low.astype(F32), supp.astype(F32), ones], axis=0),
           jnp.concatenate([upp.astype(F32), slow.astype(F32), ones], axis=0))
    sel_r = lax.broadcasted_iota(jnp.int32, (LANES, LANES), 0)
    lane = lax.broadcasted_iota(jnp.int32, (ROW_TILE, LANES), 1)

    def prep(q_ref, k_ref, v_ref, ab_ref, t_len):
        n_tiles = t_len // ROW_TILE
        for src, dst, cw_ref, kind in ((q_ref, q_s, cwq_ref, "q"), (k_ref, k_s, cwk_ref, "k"),
                                       (v_ref, v_s, cwv_ref, "v")):
            _fill_pad(pad_ref, src, t_len)
            for i in range(n_tiles):
                xc = _silu(_conv_tile(pad_ref, cw_ref, i))
                if kind != "v":
                    xc = xc * lax.rsqrt(jnp.sum(xc * xc, axis=-1, keepdims=True) + EPS)
                if kind == "q":
                    xc = xc * (DK_B ** -0.5)
                dst[i * ROW_TILE:(i + 1) * ROW_TILE, :] = xc
        for i in range(n_tiles):
            rows = slice(i * ROW_TILE, (i + 1) * ROW_TILE)
            ab = ab_ref[rows, :]
            gval = -jnp.exp(prm_ref[0:1, :]) * _softplus(ab + prm_ref[1:2, :])
            gbv = jnp.where(lane < 2 * H_B, gval, _sigmoid(ab))
            for dirn, g_dst, b_dst in ((0, gf_s, bf_s), (1, gb_s, bb_s)):
                g_dst[rows, :] = _dot_hi(gbv, (sel_r == dirn * H_B + head).astype(F32))
                b_dst[rows, :] = _dot_hi(gbv, (sel_r == 2 * H_B + dirn * H_B + head).astype(F32))

    def chunk(dirn, ci, s):
        r0 = pl.multiple_of(ci * c, c)
        rows = pl.ds(r0, c)
        q, k, v = q_s[rows, :], k_s[rows, :], v_s[rows, :]
        g = (gf_s, gb_s)[dirn][rows, :]
        beta = (bf_s, bb_s)[dirn][rows, :]
        cum = _dot_hi(mst[dirn], g)
        e_in = jnp.exp(cum[0:c])
        e_rest = jnp.exp(cum[c:2 * c])
        e_tot = jnp.exp(cum[2 * c:2 * c + 1])
        dec = jnp.exp(_dot_hi(incl[dirn], g[:, 0:c] * strk[dirn]))
        kb = k * beta
        a = jnp.where(a_mask[dirn], _dot_nt(kb, k) * dec, 0.0)
        ap = _dot3(_split3_lhs(a), _split3_rhs(a))
        ap_r = _split3_rhs(ap)
        qsum = ap
        for _ in range(4):
            ap = _dot3(_split3_lhs(ap), ap_r)
            ap_r = _split3_rhs(ap)
            qsum = qsum + ap + _dot3(_split3_lhs(qsum), ap_r)
        ima = eye - a
        t_inv = ima + _dot3(_split3_lhs(ima), _split3_rhs(qsum))
        uw = _dot(t_inv, jnp.concatenate([v * beta, kb * e_in], axis=1))
        u, w = uw[:, :LANES], uw[:, LANES:]
        qk = jnp.where(s_mask[dirn], _dot_nt(q, k) * dec, 0.0)
        sb = s.astype(BF16)
        v_new = u - _dot(w, sb)
        o = _dot(q * e_in, sb) + _dot(qk, v_new)
        s_new = s * e_tot + _dot_tn(k * e_rest, v_new)
        return o, s_new

    def seg(q_ref, k_ref, v_ref, ab_ref, gt_ref, o_ref, states):
        t_len = q_ref.shape[0]
        n_chunks = t_len // c
        prep(q_ref, k_ref, v_ref, ab_ref, t_len)

        def body(n, st):
            o_f, s_f = chunk(0, n, st[0])
            of_s[pl.ds(pl.multiple_of(n * c, c), c), :] = o_f
            nb = n_chunks - 1 - n
            o_b, s_b = chunk(1, nb, st[1])
            o_ref[pl.ds(pl.multiple_of(nb * c, c), c), :] = o_b
            return s_f, s_b

        states = lax.fori_loop(0, n_chunks, body, states)
        for i in range(t_len // ROW_TILE):
            rows = slice(i * ROW_TILE, (i + 1) * ROW_TILE)
            o = of_s[rows, :] + o_ref[rows, :]
            o_ref[rows, :] = _rms(o) * nw_ref[...] * _silu(gt_ref[rows, :])
        return states

    zero = jnp.zeros((LANES, LANES), F32)
    states = seg(qc_ref, kc_ref, vc_ref, abc_ref, gtc_ref, oc_ref, (zero, zero))
    seg(qx_ref, kx_ref, vx_ref, abx_ref, gtx_ref, ox_ref, states)


def _gdn(zc, zx, cwb, prm, nw):
    bsz, tc_len, _ = zc.shape
    tx_len = zx.shape[1]

    def zspec(t_len, col, per_head=True):
        if per_head:
            return pl.BlockSpec((None, t_len, LANES), lambda b, h: (b, 0, col + h))
        return pl.BlockSpec((None, t_len, LANES), lambda b, h: (b, 0, col))

    def seg_specs(t_len):
        return [zspec(t_len, COL_Q), zspec(t_len, COL_K), zspec(t_len, COL_V),
                zspec(t_len, COL_AB, False), zspec(t_len, COL_GB)]

    def cw_spec(off):
        return pl.BlockSpec((CONV_W, LANES), lambda b, h: (0, off + h))

    sc = functools.partial(pltpu.VMEM, dtype=F32)
    return pl.pallas_call(
        _gdn_kernel,
        grid=(bsz, H_B),
        in_specs=seg_specs(tc_len) + seg_specs(tx_len) + [
            cw_spec(0), cw_spec(H_B), cw_spec(2 * H_B),
            pl.BlockSpec(prm.shape, lambda b, h: (0, 0)),
            pl.BlockSpec(nw.shape, lambda b, h: (0, 0))],
        out_specs=[pl.BlockSpec((None, tc_len, LANES), lambda b, h: (b, 0, h)),
                   pl.BlockSpec((None, tx_len, LANES), lambda b, h: (b, 0, h))],
        out_shape=[jax.ShapeDtypeStruct((bsz, tc_len, D_HEADS), F32),
                   jax.ShapeDtypeStruct((bsz, tx_len, D_HEADS), F32)],
        scratch_shapes=[pltpu.VMEM((tx_len + 2 * SUBLANES, LANES), F32)]
                       + [pltpu.VMEM((tx_len, LANES), F32) for _ in range(8)],
        compiler_params=_params(),
        name="gdn",
    )(zc, zc, zc, zc, zc, zx, zx, zx, zx, zx, cwb, cwb, cwb, prm, nw)


_LEVELS = (1, 2, 4, 8, 16, 32, 64)


def _gla_masks():
    n = TILE_C
    i = np.arange(n)[:, None]
    k = np.arange(n)[None, :]
    sums, scores = [], []
    for lv in _LEVELS:
        same = (i // (2 * lv)) == (k // (2 * lv))
        hi_i, hi_k = (i // lv) % 2, (k // lv) % 2
        sums.append(same & np.where(hi_i == 1, (hi_k == 1) & (k <= i), (hi_k == 0) & (k > i)))
        scores.append(same & (hi_i == 1) & (hi_k == 0))
    sums += [k <= i, k > i, np.ones((n, n), bool)]
    scores.append(i == k)
    sums_f = np.stack(sums).astype(np.float32)
    scores_f = np.stack(scores).astype(np.float32)
    sums_all = np.stack([sums_f, sums_f[:, ::-1, ::-1]]).reshape(2, len(sums) * n, n)
    scores_all = np.stack([scores_f, scores_f[:, ::-1, ::-1]])
    return sums_all, scores_all


def _gla_kernel(qc_ref, ffc_ref, fbc_ref, ic_ref, gtc_ref, qx_ref, ffx_ref, fbx_ref, ix_ref, gtx_ref,
                lb_ref, nw_ref, msum_ref, mscore_ref, oc_ref, ox_ref, of_s, *, layer):
    n = TILE_C
    n_lv = len(_LEVELS)
    depth = lb_ref.shape[0] // 2
    lbs = []
    for dirn in range(2):
        rows = [lb_ref[2 * j + dirn:2 * j + dirn + 1, :] for j in range(depth)]
        mx = functools.reduce(jnp.maximum, rows)
        ex = [jnp.exp(r - mx) for r in rows]
        den = functools.reduce(lambda p, q: p + q, ex)
        acc = jnp.zeros_like(mx)
        for j in range(1, layer + 1):
            acc = acc + ex[j] / den
        lbs.append(acc)

    def tile(dirn, ti, st, q_ref, f_ref, i_ref):
        rows = pl.ds(pl.multiple_of(ti * n, n), n)
        q = _silu(q_ref[rows, :])
        v = i_ref[rows, :]
        fg = lbs[dirn] + (1.0 - lbs[dirn]) * _sigmoid(f_ref[rows, :])
        k = 1.0 - fg
        lf = jnp.log(fg)
        hi = lf.astype(BF16)
        r1 = lf - hi.astype(F32)
        mid = r1.astype(BF16)
        lo = (r1 - mid.astype(F32)).astype(BF16)
        lf3 = jnp.concatenate([hi, mid, lo], axis=1)

        def csum(j):
            x3 = jnp.dot(msum_ref[dirn, j * n:(j + 1) * n, :], lf3, preferred_element_type=F32)
            return x3[:, 0:n] + x3[:, n:2 * n] + x3[:, 2 * n:3 * n]

        scores = mscore_ref[dirn, n_lv] * _dot_nt(q, k)
        for j in range(n_lv):
            e = jnp.exp(csum(j))
            scores = scores + mscore_ref[dirn, j] * _dot_nt(q * e, k * e)
        o = _dot(scores, v) + _dot_nt(q * jnp.exp(csum(n_lv)), st)
        e_tot = jnp.exp(csum(n_lv + 2)[0:1, :])
        st_new = st * e_tot + _dot_tn(v, k * jnp.exp(csum(n_lv + 1)))
        return o, st_new

    def seg(q_ref, ff_ref, fb_ref, i_ref, gt_ref, o_ref, states):
        t_len = q_ref.shape[0]
        n_tiles = t_len // n

        def body(ti, st):
            o_f, s_f = tile(0, ti, st[0], q_ref, ff_ref, i_ref)
            of_s[pl.ds(pl.multiple_of(ti * n, n), n), :] = o_f
            tb = n_tiles - 1 - ti
            o_b, s_b = tile(1, tb, st[1], q_ref, fb_ref, i_ref)
            o_ref[pl.ds(pl.multiple_of(tb * n, n), n), :] = o_b
            return s_f, s_b

        states = lax.fori_loop(0, n_tiles, body, states)
        for i in range(t_len // ROW_TILE):
            rows = slice(i * ROW_TILE, (i + 1) * ROW_TILE)
            o = of_s[rows, :] + o_ref[rows, :]
            o_ref[rows, :] = _rms(o) * nw_ref[...] * _silu(gt_ref[rows, :])
        return states

    zero = jnp.zeros((LANES, LANES), F32)
    states = seg(qc_ref, ffc_ref, fbc_ref, ic_ref, gtc_ref, oc_ref, (zero, zero))
    seg(qx_ref, ffx_ref, fbx_ref, ix_ref, gtx_ref, ox_ref, states)


def _gla(zc, zx, lb, nw, msum, mscore, layer):
    bsz, tc_len, _ = zc.shape
    tx_len = zx.shape[1]

    def zspec(t_len, col):
        return pl.BlockSpec((None, t_len, LANES), lambda b, h: (b, 0, col + h))

    def seg_specs(t_len):
        return [zspec(t_len, COL_QC), zspec(t_len, COL_FF), zspec(t_len, COL_FB),
                zspec(t_len, COL_IC), zspec(t_len, COL_GC)]

    return pl.pallas_call(
        functools.partial(_gla_kernel, layer=layer),
        grid=(bsz, H_C),
        in_specs=seg_specs(tc_len) + seg_specs(tx_len) + [
            pl.BlockSpec((lb.shape[0], LANES), lambda b, h: (0, h)),
            pl.BlockSpec(nw.shape, lambda b, h: (0, 0)),
            pl.BlockSpec(msum.shape, lambda b, h: (0, 0, 0)),
            pl.BlockSpec(mscore.shape, lambda b, h: (0, 0, 0, 0))],
        out_specs=[pl.BlockSpec((None, tc_len, LANES), lambda b, h: (b, 0, h)),
                   pl.BlockSpec((None, tx_len, LANES), lambda b, h: (b, 0, h))],
        out_shape=[jax.ShapeDtypeStruct((bsz, tc_len, D_HEADS), F32),
                   jax.ShapeDtypeStruct((bsz, tx_len, D_HEADS), F32)],
        scratch_shapes=[pltpu.VMEM((tx_len, LANES), F32)],
        compiler_params=_params(),
        name="hgrn2",
    )(zc, zc, zc, zc, zc, zx, zx, zx, zx, zx, lb, nw, msum, mscore)


def _block_diag_halves(w):
    w4 = w.reshape(2, H_A // 2, HD_A, HD_A)
    eye = jnp.eye(H_A // 2, dtype=w.dtype)
    return jnp.einsum("ghij,hk->ghikj", w4, eye).reshape(2, (H_A // 2) * HD_A, (H_A // 2) * HD_A)


def kernel(x, c, ctx, c_ctx, w_ada, b_ada, norm_pre, norm_post, w_in, conv_a_w, conv_a_b, rg_w_r, rg_b_r,
           rg_w_i, rg_b_i, rg_lam, conv_b_w, gdn_a_log, gdn_dt_bias, gdn_norm, hg_lb, hg_norm, w_out):
    bsz, t_len, d = x.shape
    depth = w_ada.shape[0]
    dh = D_HEADS

    cs = jnp.concatenate([c, c_ctx[None, :], jnp.zeros((16 - bsz - 1, d), F32)], axis=0)
    mod = _ada(cs, w_ada, b_ada)

    qkv_end = 2 * dh + 3 * dh
    w_in_p = jnp.concatenate(
        [w_in[:, :, :qkv_end], w_in[:, :, qkv_end + 4 * H_B:], w_in[:, :, qkv_end:qkv_end + 4 * H_B],
         jnp.zeros((depth, d, LANES - 4 * H_B), w_in.dtype)], axis=-1).astype(BF16)
    w_out_b = w_out.astype(BF16)
    msum_np, mscore_np = _gla_masks()
    msum = jnp.asarray(msum_np, BF16)
    mscore = jnp.asarray(mscore_np, F32)
    lb2 = hg_lb.reshape(depth * 2, dh)

    h, hc = x, ctx
    for l in range(depth):
        colmajor = l % 2 == 1
        mx = mod[l, :bsz].reshape(bsz, 1, 3 * d)
        mc = jnp.broadcast_to(mod[l, bsz].reshape(1, 1, 3 * d), (bsz, 1, 3 * d))
        npre = norm_pre[l].reshape(1, d)
        npost = norm_post[l].reshape(1, d)
        zc = _inproj(hc, mc[..., :d], mc[..., d:2 * d], npre, w_in_p[l], colmajor=False)
        zx = _inproj(h, mx[..., :d], mx[..., d:2 * d], npre, w_in_p[l], colmajor=colmajor)

        wg = jnp.concatenate([_block_diag_halves(rg_w_r[l, 0]), _block_diag_halves(rg_w_i[l, 0]),
                              _block_diag_halves(rg_w_r[l, 1]), _block_diag_halves(rg_w_i[l, 1])],
                             axis=-1).astype(BF16)
        ya_c, ya_x = _rglru(zc, zx, conv_a_w[l], conv_a_b[l].reshape(1, dh), wg,
                            rg_b_r[l], rg_b_i[l], rg_lam[l])

        prm = jnp.zeros((SUBLANES, LANES), F32)
        prm = prm.at[0, :2 * H_B].set(gdn_a_log[l].reshape(-1)).at[1, :2 * H_B].set(gdn_dt_bias[l].reshape(-1))
        yb_c, yb_x = _gdn(zc, zx, conv_b_w[l], prm, gdn_norm[l].reshape(1, LANES))

        yc_c, yc_x = _gla(zc, zx, lb2, hg_norm[l].reshape(1, LANES), msum, mscore, l)

        h = _outproj(ya_x, yb_x, yc_x, w_out_b[l], h, mx[..., 2 * d:], npost, colmajor=colmajor)
        if l < depth - 1:
            hc = _outproj(ya_c, yb_c, yc_c, w_out_b[l], hc, mc[..., 2 * d:], npost, colmajor=False)
    return h
```

```python
import functools

import numpy as np
import jax
import jax.numpy as jnp
from jax import lax
from jax.experimental import pallas as pl
from jax.experimental.pallas import tpu as pltpu

F32 = jnp.float32
BF16 = jnp.bfloat16
HI = lax.Precision.HIGHEST

GRID_W = 64
H_A, HD_A = 8, 64
H_B, DK_B = 4, 128
H_C, DK_C = 4, 128
D_HEADS = 512
CONV_W = 4
CONV_PAD = 2
RG_C = 8.0
EPS = 1e-6

LANES = 128
SUBLANES = 8
ROW_TILE = 256
CHUNK_B = 64
TILE_C = 128
VMEM_LIMIT = 56 * 1024 * 1024

COL_XA, COL_GA, COL_Q, COL_K, COL_V, COL_GB = 0, 4, 8, 12, 16, 20
COL_QC, COL_FF, COL_FB, COL_IC, COL_GC, COL_AB = 24, 28, 32, 36, 40, 44
N_PROJ = 45 * LANES


def _dot(a, b):
    return jnp.dot(a.astype(BF16), b.astype(BF16), preferred_element_type=F32)


def _dot_nt(a, b):
    return lax.dot_general(a.astype(BF16), b.astype(BF16), (((1,), (1,)), ((), ())),
                           preferred_element_type=F32)


def _dot_tn(a, b):
    return lax.dot_general(a.astype(BF16), b.astype(BF16), (((0,), (0,)), ((), ())),
                           preferred_element_type=F32)


def _dot_hi(a, b):
    return jnp.dot(a, b, precision=HI, preferred_element_type=F32)


def _dot3(lhs3, rhs3):
    return jnp.dot(lhs3, rhs3, preferred_element_type=F32)


def _sigmoid(x):
    return 1.0 / (1.0 + jnp.exp(-x))


def _silu(x):
    return x * _sigmoid(x)


def _softplus(x):
    return jnp.maximum(x, 0.0) + jnp.log1p(jnp.exp(-jnp.abs(x)))


def _rms(x):
    return x * lax.rsqrt(jnp.mean(x * x, axis=-1, keepdims=True) + EPS)


def _params(**kw):
    return pltpu.CompilerParams(vmem_limit_bytes=VMEM_LIMIT, **kw)


def _ada_kernel(s_ref, w_ref, b_ref, o_ref):
    o_ref[...] = _dot_hi(_silu(s_ref[...]), w_ref[...]) + b_ref[...]


def _ada(cs, w_ada, b_ada):
    depth, d, n3 = w_ada.shape
    tn = 1024
    return pl.pallas_call(
        _ada_kernel,
        grid=(depth, n3 // tn),
        in_specs=[pl.BlockSpec((16, d), lambda l, j: (0, 0)),
                  pl.BlockSpec((None, d, tn), lambda l, j: (l, 0, j)),
                  pl.BlockSpec((None, 1, tn), lambda l, j: (l, 0, j))],
        out_specs=pl.BlockSpec((None, 16, tn), lambda l, j: (l, 0, j)),
        out_shape=jax.ShapeDtypeStruct((depth, 16, n3), F32),
        compiler_params=_params(),
        name="ada",
    )(cs, w_ada, b_ada.reshape(depth, 1, n3))


def _inproj_kernel(h_ref, sh_ref, sc_ref, npre_ref, w_ref, z_ref, *, colmajor, cn):
    d = npre_ref.shape[-1]
    if colmajor:
        x = jnp.concatenate([h_ref[:, j * d:(j + 1) * d] for j in range(h_ref.shape[-1] // d)], axis=0)
    else:
        x = h_ref[...]
    u = (_rms(x) * npre_ref[...] * (1.0 + sc_ref[...]) + sh_ref[...]).astype(BF16)
    for j in range(z_ref.shape[-1] // cn):
        z_ref[:, j * cn:(j + 1) * cn] = jnp.dot(u, w_ref[:, j * cn:(j + 1) * cn],
                                               preferred_element_type=F32)


def _inproj(h, shift, scale, npre, w, *, colmajor):
    bsz, t_len, d = h.shape
    n = w.shape[-1]
    tc = ROW_TILE
    if colmajor:
        rows = t_len // GRID_W
        h_in = h.reshape(bsz, rows, GRID_W * d)
        h_spec = pl.BlockSpec((None, rows, (tc // rows) * d), lambda b, i: (b, 0, i))
    else:
        h_in = h
        h_spec = pl.BlockSpec((None, tc, d), lambda b, i: (b, i, 0))
    vec = pl.BlockSpec((None, 1, d), lambda b, i: (b, 0, 0))
    return pl.pallas_call(
        functools.partial(_inproj_kernel, colmajor=colmajor, cn=1152),
        grid=(bsz, t_len // tc),
        in_specs=[h_spec, vec, vec,
                  pl.BlockSpec((1, d), lambda b, i: (0, 0)),
                  pl.BlockSpec((d, n), lambda b, i: (0, 0))],
        out_specs=pl.BlockSpec((None, tc, n), lambda b, i: (b, i, 0)),
        out_shape=jax.ShapeDtypeStruct((bsz, t_len, n), F32),
        compiler_params=_params(),
        name="inproj",
    )(h_in, shift, scale, npre, w)


def _outproj_kernel(ya_ref, yb_ref, yc_ref, w_ref, h_ref, gt_ref, npost_ref, o_ref, *, colmajor):
    dm = ya_ref.shape[-1]
    d = npost_ref.shape[-1]
    o = (jnp.dot(ya_ref[...].astype(BF16), w_ref[0:dm, :], preferred_element_type=F32)
         + jnp.dot(yb_ref[...].astype(BF16), w_ref[dm:2 * dm, :], preferred_element_type=F32)
         + jnp.dot(yc_ref[...].astype(BF16), w_ref[2 * dm:3 * dm, :], preferred_element_type=F32))
    upd = gt_ref[...] * (_rms(o) * npost_ref[...])
    if colmajor:
        rows = h_ref.shape[0]
        for j in range(h_ref.shape[-1] // d):
            o_ref[:, j * d:(j + 1) * d] = h_ref[:, j * d:(j + 1) * d] + upd[j * rows:(j + 1) * rows, :]
    else:
        o_ref[...] = h_ref[...] + upd


def _outproj(ya, yb, yc, w, h, gate, npost, *, colmajor):
    bsz, t_len, d = h.shape
    dm = ya.shape[-1]
    tc = ROW_TILE
    if colmajor:
        rows = t_len // GRID_W
        h_in = h.reshape(bsz, rows, GRID_W * d)
        h_spec = pl.BlockSpec((None, rows, (tc // rows) * d), lambda b, i: (b, 0, i))
    else:
        h_in = h
        h_spec = pl.BlockSpec((None, tc, d), lambda b, i: (b, i, 0))
    y_spec = pl.BlockSpec((None, tc, dm), lambda b, i: (b, i, 0))
    out = pl.pallas_call(
        functools.partial(_outproj_kernel, colmajor=colmajor),
        grid=(bsz, t_len // tc),
        in_specs=[y_spec, y_spec, y_spec,
                  pl.BlockSpec((3 * dm, d), lambda b, i: (0, 0)),
                  h_spec,
                  pl.BlockSpec((None, 1, d), lambda b, i: (b, 0, 0)),
                  pl.BlockSpec((1, d), lambda b, i: (0, 0))],
        out_specs=h_spec,
        out_shape=jax.ShapeDtypeStruct(h_in.shape, F32),
        compiler_params=_params(),
        name="outproj",
    )(ya, yb, yc, w, h_in, gate, npost)
    return out.reshape(bsz, t_len, d)


def _conv_tile(pad_ref, cw_ref, i):
    base = i * ROW_TILE + SUBLANES - CONV_PAD
    acc = cw_ref[0:1, :] * pad_ref[base:base + ROW_TILE, :]
    for j in range(1, CONV_W):
        acc = acc + cw_ref[j:j + 1, :] * pad_ref[base + j:base + j + ROW_TILE, :]
    return acc


def _fill_pad(pad_ref, x_ref, t_len):
    zeros = jnp.zeros((SUBLANES, pad_ref.shape[-1]), F32)
    pad_ref[0:SUBLANES, :] = zeros
    pad_ref[SUBLANES:SUBLANES + t_len, :] = x_ref[...]
    pad_ref[SUBLANES + t_len:2 * SUBLANES + t_len, :] = zeros


def _rglru_kernel(xc_ref, gc_ref, xx_ref, gx_ref, cw_ref, cb_ref, wg_ref, br_ref, bi_ref, lam_ref,
                  oc_ref, ox_ref, pad_ref, a_ref, u_ref):
    dh = xc_ref.shape[-1]
    half = dh // 2
    n_grp = ROW_TILE // SUBLANES
    sp = _softplus(-lam_ref[...])
    row = lax.broadcasted_iota(jnp.int32, (ROW_TILE, dh), 0) % SUBLANES

    def gates(i, dirn):
        xc = _conv_tile(pad_ref, cw_ref, i) + cb_ref[...]
        pre = [_dot(xc[:, hf * half:(hf + 1) * half], wg_ref[hf, :, dirn * dh:(dirn + 1) * dh])
               for hf in range(2)]
        r_pre = jnp.concatenate([pre[0][:, :half], pre[1][:, :half]], axis=1)
        i_pre = jnp.concatenate([pre[0][:, half:], pre[1][:, half:]], axis=1)
        r = _sigmoid(r_pre + br_ref[dirn:dirn + 1, :])
        ig = _sigmoid(i_pre + bi_ref[dirn:dirn + 1, :])
        log_a = -RG_C * r * sp[dirn:dirn + 1, :]
        a = jnp.exp(log_a)
        u = jnp.sqrt(-jnp.tanh(log_a) * (a * a + 1.0)) * (ig * xc)
        return a, u

    def seg(x_ref, g_ref, o_ref, t_len, h0):
        n_tiles = t_len // ROW_TILE
        _fill_pad(pad_ref, x_ref, t_len)
        carry = h0[0]
        for i in range(n_tiles):
            a, u = gates(i, 0)
            for sft in (1, 2, 4):
                keep = row >= sft
                u = u + a * jnp.where(keep, pltpu.roll(u, sft, 0), 0.0)
                a = a * jnp.where(keep, pltpu.roll(a, sft, 0), 1.0)
            a_ref[...] = a
            u_ref[...] = u

            def fbody(g, cr, i=i):
                r0 = pl.multiple_of(g * SUBLANES, SUBLANES)
                hh = u_ref[pl.ds(r0, SUBLANES), :] + a_ref[pl.ds(r0, SUBLANES), :] * cr
                o_ref[pl.ds(i * ROW_TILE + r0, SUBLANES), :] = hh
                return hh[SUBLANES - 1:SUBLANES, :]

            carry = lax.fori_loop(0, n_grp, fbody, carry)
        hf_end = carry
        carry = h0[1]
        for i in reversed(range(n_tiles)):
            a, u = gates(i, 1)
            for sft in (1, 2, 4):
                keep = row < SUBLANES - sft
                u = u + a * jnp.where(keep, pltpu.roll(u, ROW_TILE - sft, 0), 0.0)
                a = a * jnp.where(keep, pltpu.roll(a, ROW_TILE - sft, 0), 1.0)
            a_ref[...] = a
            u_ref[...] = u

            def bbody(gi, cr, i=i):
                r0 = pl.multiple_of((n_grp - 1 - gi) * SUBLANES, SUBLANES)
                hh = u_ref[pl.ds(r0, SUBLANES), :] + a_ref[pl.ds(r0, SUBLANES), :] * cr
                rows = pl.ds(i * ROW_TILE + r0, SUBLANES)
                o_ref[rows, :] = (o_ref[rows, :] + hh) * _silu(g_ref[rows, :])
                return hh[0:1, :]

            carry = lax.fori_loop(0, n_grp, bbody, carry)
        return hf_end, carry

    zero = jnp.zeros((1, dh), F32)
    states = seg(xc_ref, gc_ref, oc_ref, xc_ref.shape[0], (zero, zero))
    seg(xx_ref, gx_ref, ox_ref, xx_ref.shape[0], states)


def _rglru(zc, zx, cw, cb, wg, br, bi, lam):
    bsz, tc_len, _ = zc.shape
    tx_len = zx.shape[1]
    dh = D_HEADS
    nb = dh // LANES

    def zspec(t_len, col):
        return pl.BlockSpec((None, t_len, dh), lambda b: (b, 0, col // nb))

    def full(a):
        return pl.BlockSpec(a.shape, lambda b: (0,) * a.ndim)

    return pl.pallas_call(
        _rglru_kernel,
        grid=(bsz,),
        in_specs=[zspec(tc_len, COL_XA), zspec(tc_len, COL_GA), zspec(tx_len, COL_XA), zspec(tx_len, COL_GA),
                  full(cw), full(cb), full(wg), full(br), full(bi), full(lam)],
        out_specs=[pl.BlockSpec((None, tc_len, dh), lambda b: (b, 0, 0)),
                   pl.BlockSpec((None, tx_len, dh), lambda b: (b, 0, 0))],
        out_shape=[jax.ShapeDtypeStruct((bsz, tc_len, dh), F32),
                   jax.ShapeDtypeStruct((bsz, tx_len, dh), F32)],
        scratch_shapes=[pltpu.VMEM((tx_len + 2 * SUBLANES, dh), F32),
                        pltpu.VMEM((ROW_TILE, dh), F32),
                        pltpu.VMEM((ROW_TILE, dh), F32)],
        compiler_params=_params(),
        name="rglru",
    )(zc, zc, zx, zx, cw, cb, wg, br, bi, lam)


GDN_UNROLL = 4


def _gdn_masks():
    c = CHUNK_B
    i = np.arange(c)[:, None]
    k = np.arange(c)[None, :]
    ones = np.ones((c, c), bool)
    stacked = np.stack([np.concatenate([k <= i, k > i, ones], 0), np.concatenate([k >= i, k < i, ones], 0)])
    incl = np.stack([k <= i, k >= i])
    return np.tile(stacked, (1, 1, 3)).astype(np.float32), np.tile(incl, (1, 1, 3)).astype(np.float32)


def _split3_rows(x):
    hi = x.astype(BF16)
    r1 = x - hi.astype(F32)
    mid = r1.astype(BF16)
    lo = (r1 - mid.astype(F32)).astype(BF16)
    return jnp.concatenate([hi, mid, lo], axis=0)


def _pair_lhs(x2):
    hi = x2.astype(BF16)
    lo = (x2 - hi.astype(F32)).astype(BF16)
    return jnp.concatenate([hi, lo], axis=1)


def _pair_rhs(x2):
    hi = x2.astype(BF16)
    lo = (x2 - hi.astype(F32)).astype(BF16)
    return jnp.concatenate([hi, lo, hi, lo], axis=0)


def _gdn_kernel(qc_ref, kc_ref, vc_ref, abc_ref, gtc_ref, qx_ref, kx_ref, vx_ref, abx_ref, gtx_ref,
                cwq_ref, cwk_ref, cwv_ref, prm_ref, nw_ref, mst_ref, incl_ref, oc_ref, ox_ref,
                pad_ref, q_s, k_s, v_s, gf_s, bf_s, gb_s, bb_s, o_s, qp_s, p_s, n_s, et_s):
    head = pl.program_id(1)
    c = CHUNK_B
    ii = lax.broadcasted_iota(jnp.int32, (c, LANES), 0)
    jd = lax.broadcasted_iota(jnp.int32, (c, LANES), 1) % c
    eye2 = (ii == jd).astype(F32)
    a_mask2 = (ii > jd, ii < jd)
    strk2 = (a_mask2[0].astype(F32), a_mask2[1].astype(F32))
    s_mask = ((ii >= jd)[:, :c], (ii <= jd)[:, :c])
    sel_r = lax.broadcasted_iota(jnp.int32, (LANES, LANES), 0)
    lane = lax.broadcasted_iota(jnp.int32, (ROW_TILE, LANES), 1)

    def prep(q_ref, k_ref, v_ref, ab_ref, t_len):
        n_tiles = t_len // ROW_TILE
        for src, dst, cw_ref, kind in ((q_ref, q_s, cwq_ref, "q"), (k_ref, k_s, cwk_ref, "k"),
                                       (v_ref, v_s, cwv_ref, "v")):
            _fill_pad(pad_ref, src, t_len)
            for i in range(n_tiles):
                xc = _silu(_conv_tile(pad_ref, cw_ref, i))
                if kind != "v":
                    xc = xc * lax.rsqrt(jnp.sum(xc * xc, axis=-1, keepdims=True) + EPS)
                if kind == "q":
                    xc = xc * (DK_B ** -0.5)
                dst[i * ROW_TILE:(i + 1) * ROW_TILE, :] = xc
        for i in range(n_tiles):
            rows = slice(i * ROW_TILE, (i + 1) * ROW_TILE)
            ab = ab_ref[rows, :]
            gval = -jnp.exp(prm_ref[0:1, :]) * _softplus(ab + prm_ref[1:2, :])
            gbv = jnp.where(lane < 2 * H_B, gval, _sigmoid(ab))
            for dirn, g_dst, b_dst in ((0, gf_s, bf_s), (1, gb_s, bb_s)):
                g_dst[rows, :] = _dot_hi(gbv, (sel_r == dirn * H_B + head).astype(F32))
                b_dst[rows, :] = _dot_hi(gbv, (sel_r == 2 * H_B + dirn * H_B + head).astype(F32))

    def chunk_terms(items):
        m = range(len(items))
        dirs = [d for d, _ in items]
        rows = [pl.ds(pl.multiple_of(ci * c, c), c) for _, ci in items]
        prow = [pl.ds(pl.multiple_of(ci * LANES, LANES), LANES) for _, ci in items]
        erow = [pl.ds(pl.multiple_of(ci * SUBLANES, SUBLANES), SUBLANES) for _, ci in items]
        q = [q_s[r, :] for r in rows]
        k = [k_s[r, :] for r in rows]
        v = [v_s[r, :] for r in rows]
        g = [(gf_s, gb_s)[d][r, :] for d, r in zip(dirs, rows)]
        beta = [(bf_s, bb_s)[d][r, :] for d, r in zip(dirs, rows)]
        cum = [jnp.dot(mst_ref[dirs[i]], _split3_rows(g[i]), preferred_element_type=F32) for i in m]
        dlt = [jnp.dot(incl_ref[dirs[i]], _split3_rows(g[i] * strk2[dirs[i]]), preferred_element_type=F32)
               for i in m]
        kb = [k[i] * beta[i] for i in m]
        kk2 = [_dot_nt(kb[i], jnp.concatenate([k[i], k[i]], axis=0)) for i in m]
        qk0 = [_dot_nt(q[i], k[i]) for i in m]
        e_in = [jnp.exp(cum[i][0:c]) for i in m]
        e_rest = [jnp.exp(cum[i][c:2 * c]) for i in m]
        dec2 = [jnp.exp(dlt[i]) for i in m]
        for i in m:
            et_s[dirs[i], erow[i], :] = jnp.exp(cum[i][2 * c:2 * c + SUBLANES])
        a2 = [jnp.where(a_mask2[dirs[i]], kk2[i] * dec2[i], 0.0) for i in m]
        qk = [jnp.where(s_mask[dirs[i]], qk0[i] * dec2[i][:, :c], 0.0) for i in m]
        ap = [_dot3(_pair_lhs(a2[i]), _pair_rhs(a2[i])) for i in m]
        ap_r = [_pair_rhs(ap[i]) for i in m]
        qsum = ap
        for _ in range(4):
            ap = [_dot3(_pair_lhs(ap[i]), ap_r[i]) for i in m]
            ap_r = [_pair_rhs(ap[i]) for i in m]
            qsum = [qsum[i] + ap[i] + _dot3(_pair_lhs(qsum[i]), ap_r[i]) for i in m]
        ima = [eye2 - a2[i] for i in m]
        t_inv = [(ima[i] + _dot3(_pair_lhs(ima[i]), _pair_rhs(qsum[i])))[:, :c] for i in m]
        uw = [_dot(t_inv[i], jnp.concatenate([v[i] * beta[i], kb[i] * e_in[i]], axis=1)) for i in m]
        np_ = [_dot_tn(k[i] * e_rest[i], uw[i]) for i in m]
        oq = [_dot(qk[i], uw[i]) for i in m]
        for i in m:
            n_s[dirs[i], prow[i], :] = np_[i][:, :LANES]
            p_s[dirs[i], prow[i], :] = np_[i][:, LANES:].astype(BF16)
            o_s[dirs[i], rows[i], :] = oq[i][:, :LANES]
            qp_s[dirs[i], rows[i], :] = (q[i] * e_in[i] - oq[i][:, LANES:]).astype(BF16)

    def seg(q_ref, k_ref, v_ref, ab_ref, gt_ref, o_ref, states):
        t_len = q_ref.shape[0]
        n_chunks = t_len // c
        prep(q_ref, k_ref, v_ref, ab_ref, t_len)

        def terms_body(it, carry):
            chunk_terms([(dirn, it * GDN_UNROLL + j) for j in range(GDN_UNROLL) for dirn in range(2)])
            return carry

        lax.fori_loop(0, n_chunks // GDN_UNROLL, terms_body, 0)

        def state_body(n, st):
            new = []
            for dirn, ci in ((0, n), (1, n_chunks - 1 - n)):
                rows = pl.ds(pl.multiple_of(ci * c, c), c)
                prow = pl.ds(pl.multiple_of(ci * LANES, LANES), LANES)
                sb = st[dirn].astype(BF16)
                o_s[dirn, rows, :] = o_s[dirn, rows, :] + jnp.dot(qp_s[dirn, rows, :], sb,
                                                                 preferred_element_type=F32)
                e_tot = et_s[dirn, pl.ds(pl.multiple_of(ci * SUBLANES, SUBLANES), 1), :]
                new.append(st[dirn] * e_tot + n_s[dirn, prow, :]
                           - jnp.dot(p_s[dirn, prow, :], sb, preferred_element_type=F32))
            return tuple(new)

        states = lax.fori_loop(0, n_chunks, state_body, states)
        for i in range(t_len // ROW_TILE):
            rows = slice(i * ROW_TILE, (i + 1) * ROW_TILE)
            o = o_s[0, rows, :] + o_s[1, rows, :]
            o_ref[rows, :] = _rms(o) * nw_ref[...] * _silu(gt_ref[rows, :])
        return states

    zero = jnp.zeros((LANES, LANES), F32)
    states = seg(qc_ref, kc_ref, vc_ref, abc_ref, gtc_ref, oc_ref, (zero, zero))
    seg(qx_ref, kx_ref, vx_ref, abx_ref, gtx_ref, ox_ref, states)


def _gdn(zc, zx, cwb, prm, nw):
    bsz, tc_len, _ = zc.shape
    tx_len = zx.shape[1]

    def zspec(t_len, col, per_head=True):
        if per_head:
            return pl.BlockSpec((None, t_len, LANES), lambda b, h: (b, 0, col + h))
        return pl.BlockSpec((None, t_len, LANES), lambda b, h: (b, 0, col))

    def seg_specs(t_len):
        return [zspec(t_len, COL_Q), zspec(t_len, COL_K), zspec(t_len, COL_V),
                zspec(t_len, COL_AB, False), zspec(t_len, COL_GB)]

    def cw_spec(off):
        return pl.BlockSpec((CONV_W, LANES), lambda b, h: (0, off + h))

    mst_np, incl_np = _gdn_masks()
    mst, incl = jnp.asarray(mst_np, BF16), jnp.asarray(incl_np, BF16)
    n_chunks = tx_len // CHUNK_B
    return pl.pallas_call(
        _gdn_kernel,
        grid=(bsz, H_B),
        in_specs=seg_specs(tc_len) + seg_specs(tx_len) + [
            cw_spec(0), cw_spec(H_B), cw_spec(2 * H_B),
            pl.BlockSpec(prm.shape, lambda b, h: (0, 0)),
            pl.BlockSpec(nw.shape, lambda b, h: (0, 0)),
            pl.BlockSpec(mst.shape, lambda b, h: (0, 0, 0)),
            pl.BlockSpec(incl.shape, lambda b, h: (0, 0, 0))],
        out_specs=[pl.BlockSpec((None, tc_len, LANES), lambda b, h: (b, 0, h)),
                   pl.BlockSpec((None, tx_len, LANES), lambda b, h: (b, 0, h))],
        out_shape=[jax.ShapeDtypeStruct((bsz, tc_len, D_HEADS), F32),
                   jax.ShapeDtypeStruct((bsz, tx_len, D_HEADS), F32)],
        scratch_shapes=[pltpu.VMEM((tx_len + 2 * SUBLANES, LANES), F32)]
                       + [pltpu.VMEM((tx_len, LANES), F32) for _ in range(7)]
                       + [pltpu.VMEM((2, tx_len, LANES), F32),
                          pltpu.VMEM((2, tx_len, LANES), BF16),
                          pltpu.VMEM((2, n_chunks * LANES, LANES), BF16),
                          pltpu.VMEM((2, n_chunks * LANES, LANES), F32),
                          pltpu.VMEM((2, n_chunks * SUBLANES, LANES), F32)],
        compiler_params=_params(),
        name="gdn",
    )(zc, zc, zc, zc, zc, zx, zx, zx, zx, zx, cwb, cwb, cwb, prm, nw, mst, incl)


_LEVELS = (1, 2, 4, 8, 16, 32, 64)


def _gla_masks():
    n = TILE_C
    i = np.arange(n)[:, None]
    k = np.arange(n)[None, :]
    sums, scores = [], []
    for lv in _LEVELS:
        same = (i // (2 * lv)) == (k // (2 * lv))
        hi_i, hi_k = (i // lv) % 2, (k // lv) % 2
        sums.append(same & np.where(hi_i == 1, (hi_k == 1) & (k <= i), (hi_k == 0) & (k > i)))
        scores.append(same & (hi_i == 1) & (hi_k == 0))
    sums += [k <= i, k > i]
    scores.append(i == k)
    sums_f = np.stack(sums).astype(np.float32)
    scores_f = np.stack(scores).astype(np.float32)
    sums_all = np.stack([sums_f, sums_f[:, ::-1, ::-1]]).reshape(2, len(sums) * n, n)
    scores_all = np.stack([scores_f, scores_f[:, ::-1, ::-1]])
    return np.tile(sums_all, (1, 1, 2)), scores_all


GLA_UNROLL = 2


def _gla_kernel(qc_ref, ffc_ref, fbc_ref, ic_ref, gtc_ref, qx_ref, ffx_ref, fbx_ref, ix_ref, gtx_ref,
                lb_ref, nw_ref, msum_ref, mscore_ref, oc_ref, ox_ref, oi_s, qd_s, u_s, et_s, *, layer):
    n = TILE_C
    n_lv = len(_LEVELS)
    depth = lb_ref.shape[0] // 2
    lbs = []
    for dirn in range(2):
        rows = [lb_ref[2 * j + dirn:2 * j + dirn + 1, :] for j in range(depth)]
        mx = functools.reduce(jnp.maximum, rows)
        ex = [jnp.exp(r - mx) for r in rows]
        den = functools.reduce(lambda p, q: p + q, ex)
        acc = jnp.zeros_like(mx)
        for j in range(1, layer + 1):
            acc = acc + ex[j] / den
        lbs.append(acc)

    def tile_terms(items, q_ref, f_refs, i_ref):
        m = range(len(items))
        dirs = [d for d, _ in items]
        rows = [pl.ds(pl.multiple_of(ti * n, n), n) for _, ti in items]
        q = [_silu(q_ref[r, :]) for r in rows]
        v = [i_ref[r, :] for r in rows]
        fg = [lbs[dirs[i]] + (1.0 - lbs[dirs[i]]) * _sigmoid(f_refs[dirs[i]][rows[i], :]) for i in m]
        k = [1.0 - fg[i] for i in m]
        lf = [jnp.log(fg[i]) for i in m]
        hi = [lf[i].astype(BF16) for i in m]
        lf2 = [jnp.concatenate([hi[i], (lf[i] - hi[i].astype(F32)).astype(BF16)], axis=0) for i in m]

        def csum(j):
            return [jnp.dot(msum_ref[dirs[i], j * n:(j + 1) * n, :], lf2[i], preferred_element_type=F32)
                    for i in m]

        scores = [mscore_ref[dirs[i], n_lv] * _dot_nt(q[i], k[i]) for i in m]
        for j in range(n_lv):
            e = [jnp.exp(x) for x in csum(j)]
            p = [_dot_nt(q[i] * e[i], k[i] * e[i]) for i in m]
            scores = [scores[i] + mscore_ref[dirs[i], j] * p[i] for i in m]
        c_in, c_rest = csum(n_lv), csum(n_lv + 1)
        o_intra = [_dot(scores[i], v[i]) for i in m]
        u = [_dot_tn(v[i], k[i] * jnp.exp(c_rest[i])) for i in m]
        for i, (_, ti) in enumerate(items):
            oi_s[dirs[i], rows[i], :] = o_intra[i]
            qd_s[dirs[i], rows[i], :] = (q[i] * jnp.exp(c_in[i])).astype(BF16)
            u_s[dirs[i], rows[i], :] = u[i]
            et_s[dirs[i], pl.ds(pl.multiple_of(ti * SUBLANES, SUBLANES), SUBLANES), :] = jnp.exp(
                (c_in[i] + c_rest[i])[0:SUBLANES, :])

    def seg(q_ref, ff_ref, fb_ref, i_ref, gt_ref, o_ref, states):
        t_len = q_ref.shape[0]
        n_tiles = t_len // n

        def terms_body(it, carry):
            tile_terms([(dirn, it * GLA_UNROLL + j) for j in range(GLA_UNROLL) for dirn in range(2)],
                       q_ref, (ff_ref, fb_ref), i_ref)
            return carry

        lax.fori_loop(0, n_tiles // GLA_UNROLL, terms_body, 0)

        def state_body(ti, st):
            new = []
            for dirn, tj in ((0, ti), (1, n_tiles - 1 - ti)):
                rows = pl.ds(pl.multiple_of(tj * n, n), n)
                oi_s[dirn, rows, :] = oi_s[dirn, rows, :] + lax.dot_general(
                    qd_s[dirn, rows, :], st[dirn].astype(BF16), (((1,), (1,)), ((), ())),
                    preferred_element_type=F32)
                e_tot = et_s[dirn, pl.ds(pl.multiple_of(tj * SUBLANES, SUBLANES), 1), :]
                new.append(st[dirn] * e_tot + u_s[dirn, rows, :])
            return tuple(new)

        states = lax.fori_loop(0, n_tiles, state_body, states)
        for i in range(t_len // ROW_TILE):
            rows = slice(i * ROW_TILE, (i + 1) * ROW_TILE)
            o = oi_s[0, rows, :] + oi_s[1, rows, :]
            o_ref[rows, :] = _rms(o) * nw_ref[...] * _silu(gt_ref[rows, :])
        return states

    zero = jnp.zeros((LANES, LANES), F32)
    states = seg(qc_ref, ffc_ref, fbc_ref, ic_ref, gtc_ref, oc_ref, (zero, zero))
    seg(qx_ref, ffx_ref, fbx_ref, ix_ref, gtx_ref, ox_ref, states)


def _gla(zc, zx, lb, nw, msum, mscore, layer):
    bsz, tc_len, _ = zc.shape
    tx_len = zx.shape[1]

    def zspec(t_len, col):
        return pl.BlockSpec((None, t_len, LANES), lambda b, h: (b, 0, col + h))

    def seg_specs(t_len):
        return [zspec(t_len, COL_QC), zspec(t_len, COL_FF), zspec(t_len, COL_FB),
                zspec(t_len, COL_IC), zspec(t_len, COL_GC)]

    return pl.pallas_call(
        functools.partial(_gla_kernel, layer=layer),
        grid=(bsz, H_C),
        in_specs=seg_specs(tc_len) + seg_specs(tx_len) + [
            pl.BlockSpec((lb.shape[0], LANES), lambda b, h: (0, h)),
            pl.BlockSpec(nw.shape, lambda b, h: (0, 0)),
            pl.BlockSpec(msum.shape, lambda b, h: (0, 0, 0)),
            pl.BlockSpec(mscore.shape, lambda b, h: (0, 0, 0, 0))],
        out_specs=[pl.BlockSpec((None, tc_len, LANES), lambda b, h: (b, 0, h)),
                   pl.BlockSpec((None, tx_len, LANES), lambda b, h: (b, 0, h))],
        out_shape=[jax.ShapeDtypeStruct((bsz, tc_len, D_HEADS), F32),
                   jax.ShapeDtypeStruct((bsz, tx_len, D_HEADS), F32)],
        scratch_shapes=[pltpu.VMEM((2, tx_len, LANES), F32),
                        pltpu.VMEM((2, tx_len, LANES), BF16),
                        pltpu.VMEM((2, tx_len, LANES), F32),
                        pltpu.VMEM((2, (tx_len // TILE_C) * SUBLANES, LANES), F32)],
        compiler_params=_params(),
        name="hgrn2",
    )(zc, zc, zc, zc, zc, zx, zx, zx, zx, zx, lb, nw, msum, mscore)


def _block_diag_halves(w):
    w4 = w.reshape(2, H_A // 2, HD_A, HD_A)
    eye = jnp.eye(H_A // 2, dtype=w.dtype)
    return jnp.einsum("ghij,hk->ghikj", w4, eye).reshape(2, (H_A // 2) * HD_A, (H_A // 2) * HD_A)


def kernel(x, c, ctx, c_ctx, w_ada, b_ada, norm_pre, norm_post, w_in, conv_a_w, conv_a_b, rg_w_r, rg_b_r,
           rg_w_i, rg_b_i, rg_lam, conv_b_w, gdn_a_log, gdn_dt_bias, gdn_norm, hg_lb, hg_norm, w_out):
    bsz, t_len, d = x.shape
    depth = w_ada.shape[0]
    dh = D_HEADS

    cs = jnp.concatenate([c, c_ctx[None, :], jnp.zeros((16 - bsz - 1, d), F32)], axis=0)
    mod = _ada(cs, w_ada, b_ada)

    qkv_end = 2 * dh + 3 * dh
    w_in_p = jnp.concatenate(
        [w_in[:, :, :qkv_end], w_in[:, :, qkv_end + 4 * H_B:], w_in[:, :, qkv_end:qkv_end + 4 * H_B],
         jnp.zeros((depth, d, LANES - 4 * H_B), w_in.dtype)], axis=-1).astype(BF16)
    w_out_b = w_out.astype(BF16)
    msum_np, mscore_np = _gla_masks()
    msum = jnp.asarray(msum_np, BF16)
    mscore = jnp.asarray(mscore_np, F32)
    lb2 = hg_lb.reshape(depth * 2, dh)

    h, hc = x, ctx
    for l in range(depth):
        colmajor = l % 2 == 1
        mx = mod[l, :bsz].reshape(bsz, 1, 3 * d)
        mc = jnp.broadcast_to(mod[l, bsz].reshape(1, 1, 3 * d), (bsz, 1, 3 * d))
        npre = norm_pre[l].reshape(1, d)
        npost = norm_post[l].reshape(1, d)
        zc = _inproj(hc, mc[..., :d], mc[..., d:2 * d], npre, w_in_p[l], colmajor=False)
        zx = _inproj(h, mx[..., :d], mx[..., d:2 * d], npre, w_in_p[l], colmajor=colmajor)

        wg = jnp.concatenate([_block_diag_halves(rg_w_r[l, 0]), _block_diag_halves(rg_w_i[l, 0]),
                              _block_diag_halves(rg_w_r[l, 1]), _block_diag_halves(rg_w_i[l, 1])],
                             axis=-1).astype(BF16)
        ya_c, ya_x = _rglru(zc, zx, conv_a_w[l], conv_a_b[l].reshape(1, dh), wg,
                            rg_b_r[l], rg_b_i[l], rg_lam[l])

        prm = jnp.zeros((SUBLANES, LANES), F32)
        prm = prm.at[0, :2 * H_B].set(gdn_a_log[l].reshape(-1)).at[1, :2 * H_B].set(gdn_dt_bias[l].reshape(-1))
        yb_c, yb_x = _gdn(zc, zx, conv_b_w[l], prm, gdn_norm[l].reshape(1, LANES))

        yc_c, yc_x = _gla(zc, zx, lb2, hg_norm[l].reshape(1, LANES), msum, mscore, l)

        h = _outproj(ya_x, yb_x, yc_x, w_out_b[l], h, mx[..., 2 * d:], npost, colmajor=colmajor)
        if l < depth - 1:
            hc = _outproj(ya_c, yb_c, yc_c, w_out_b[l], hc, mc[..., 2 * d:], npost, colmajor=False)
    return h
```

```python
import functools

import numpy as np
import jax
import jax.numpy as jnp
from jax import lax
from jax.experimental import pallas as pl
from jax.experimental.pallas import tpu as pltpu

F32 = jnp.float32
BF16 = jnp.bfloat16
HI = lax.Precision.HIGHEST

GRID_W = 64
H_A, HD_A = 8, 64
H_B, DK_B = 4, 128
H_C, DK_C = 4, 128
D_HEADS = 512
CONV_W = 4
CONV_PAD = 2
RG_C = 8.0
EPS = 1e-6

LANES = 128
SUBLANES = 8
ROW_TILE = 256
CHUNK_B = 64
TILE_C = 128
VMEM_LIMIT = 56 * 1024 * 1024

COL_XA, COL_GA, COL_Q, COL_K, COL_V, COL_GB = 0, 4, 8, 12, 16, 20
COL_QC, COL_FF, COL_FB, COL_IC, COL_GC, COL_AB = 24, 28, 32, 36, 40, 44
N_PROJ = 45 * LANES


def _dot(a, b):
    return jnp.dot(a.astype(BF16), b.astype(BF16), preferred_element_type=F32)


def _dot_nt(a, b):
    return lax.dot_general(a.astype(BF16), b.astype(BF16), (((1,), (1,)), ((), ())),
                           preferred_element_type=F32)


def _dot_tn(a, b):
    return lax.dot_general(a.astype(BF16), b.astype(BF16), (((0,), (0,)), ((), ())),
                           preferred_element_type=F32)


def _dot_hi(a, b):
    return jnp.dot(a, b, precision=HI, preferred_element_type=F32)


def _dot3(lhs3, rhs3):
    return jnp.dot(lhs3, rhs3, preferred_element_type=F32)


def _sigmoid(x):
    return 1.0 / (1.0 + jnp.exp(-x))


def _silu(x):
    return x * _sigmoid(x)


def _softplus(x):
    return jnp.maximum(x, 0.0) + jnp.log1p(jnp.exp(-jnp.abs(x)))


def _rms(x):
    return x * lax.rsqrt(jnp.mean(x * x, axis=-1, keepdims=True) + EPS)


def _params(**kw):
    return pltpu.CompilerParams(vmem_limit_bytes=VMEM_LIMIT, **kw)


def _ada_kernel(s_ref, w_ref, b_ref, o_ref):
    o_ref[...] = _dot_hi(_silu(s_ref[...]), w_ref[...]) + b_ref[...]


def _ada(cs, w_ada, b_ada):
    depth, d, n3 = w_ada.shape
    tn = 1024
    return pl.pallas_call(
        _ada_kernel,
        grid=(depth, n3 // tn),
        in_specs=[pl.BlockSpec((16, d), lambda l, j: (0, 0)),
                  pl.BlockSpec((None, d, tn), lambda l, j: (l, 0, j)),
                  pl.BlockSpec((None, 1, tn), lambda l, j: (l, 0, j))],
        out_specs=pl.BlockSpec((None, 16, tn), lambda l, j: (l, 0, j)),
        out_shape=jax.ShapeDtypeStruct((depth, 16, n3), F32),
        compiler_params=_params(),
        name="ada",
    )(cs, w_ada, b_ada.reshape(depth, 1, n3))


def _inproj_kernel(h_ref, sh_ref, sc_ref, npre_ref, w_ref, z_ref, *, colmajor, cn):
    if colmajor:
        x = jnp.concatenate([h_ref[:, j, :] for j in range(h_ref.shape[1])], axis=0)
    else:
        x = h_ref[...]
    u = (_rms(x) * npre_ref[...] * (1.0 + sc_ref[...]) + sh_ref[...]).astype(BF16)
    for j in range(z_ref.shape[-1] // cn):
        z_ref[:, j * cn:(j + 1) * cn] = jnp.dot(u, w_ref[:, j * cn:(j + 1) * cn],
                                               preferred_element_type=F32)


def _inproj(h, shift, scale, npre, w, *, colmajor):
    bsz, t_len, d = h.shape
    n = w.shape[-1]
    tc = ROW_TILE
    if colmajor:
        rows = t_len // GRID_W
        h_in = h.reshape(bsz, rows, GRID_W, d)
        h_spec = pl.BlockSpec((None, rows, tc // rows, d), lambda b, i: (b, 0, i, 0))
    else:
        h_in = h
        h_spec = pl.BlockSpec((None, tc, d), lambda b, i: (b, i, 0))
    vec = pl.BlockSpec((None, 1, d), lambda b, i: (b, 0, 0))
    return pl.pallas_call(
        functools.partial(_inproj_kernel, colmajor=colmajor, cn=1152),
        grid=(bsz, t_len // tc),
        in_specs=[h_spec, vec, vec,
                  pl.BlockSpec((1, d), lambda b, i: (0, 0)),
                  pl.BlockSpec((d, n), lambda b, i: (0, 0))],
        out_specs=pl.BlockSpec((None, tc, n), lambda b, i: (b, i, 0)),
        out_shape=jax.ShapeDtypeStruct((bsz, t_len, n), F32),
        compiler_params=_params(),
        name="inproj",
    )(h_in, shift, scale, npre, w)


def _outproj_kernel(ya_ref, yb_ref, yc_ref, w_ref, h_ref, gt_ref, npost_ref, o_ref, *, colmajor):
    dm = ya_ref.shape[-1]
    d = npost_ref.shape[-1]
    o = (jnp.dot(ya_ref[...].astype(BF16), w_ref[0:dm, :], preferred_element_type=F32)
         + jnp.dot(yb_ref[...].astype(BF16), w_ref[dm:2 * dm, :], preferred_element_type=F32)
         + jnp.dot(yc_ref[...].astype(BF16), w_ref[2 * dm:3 * dm, :], preferred_element_type=F32))
    upd = gt_ref[...] * (_rms(o) * npost_ref[...])
    if colmajor:
        rows = h_ref.shape[0]
        for j in range(h_ref.shape[1]):
            o_ref[:, j, :] = h_ref[:, j, :] + upd[j * rows:(j + 1) * rows, :]
    else:
        o_ref[...] = h_ref[...] + upd


def _outproj(ya, yb, yc, w, h, gate, npost, *, colmajor):
    bsz, t_len, d = h.shape
    dm = ya.shape[-1]
    tc = ROW_TILE
    if colmajor:
        rows = t_len // GRID_W
        h_in = h.reshape(bsz, rows, GRID_W, d)
        h_spec = pl.BlockSpec((None, rows, tc // rows, d), lambda b, i: (b, 0, i, 0))
    else:
        h_in = h
        h_spec = pl.BlockSpec((None, tc, d), lambda b, i: (b, i, 0))
    y_spec = pl.BlockSpec((None, tc, dm), lambda b, i: (b, i, 0))
    out = pl.pallas_call(
        functools.partial(_outproj_kernel, colmajor=colmajor),
        grid=(bsz, t_len // tc),
        in_specs=[y_spec, y_spec, y_spec,
                  pl.BlockSpec((3 * dm, d), lambda b, i: (0, 0)),
                  h_spec,
                  pl.BlockSpec((None, 1, d), lambda b, i: (b, 0, 0)),
                  pl.BlockSpec((1, d), lambda b, i: (0, 0))],
        out_specs=h_spec,
        out_shape=jax.ShapeDtypeStruct(h_in.shape, F32),
        compiler_params=_params(),
        name="outproj",
    )(ya, yb, yc, w, h_in, gate, npost)
    return out.reshape(bsz, t_len, d)


def _conv_tile(pad_ref, cw_ref, i):
    base = i * ROW_TILE + SUBLANES - CONV_PAD
    acc = cw_ref[0:1, :] * pad_ref[base:base + ROW_TILE, :]
    for j in range(1, CONV_W):
        acc = acc + cw_ref[j:j + 1, :] * pad_ref[base + j:base + j + ROW_TILE, :]
    return acc


def _fill_pad(pad_ref, x_ref, t_len):
    zeros = jnp.zeros((SUBLANES, pad_ref.shape[-1]), F32)
    pad_ref[0:SUBLANES, :] = zeros
    pad_ref[SUBLANES:SUBLANES + t_len, :] = x_ref[...]
    pad_ref[SUBLANES + t_len:2 * SUBLANES + t_len, :] = zeros


def _rglru_kernel(xc_ref, gc_ref, xx_ref, gx_ref, cw_ref, cb_ref, wg_ref, br_ref, bi_ref, lam_ref,
                  oc_ref, ox_ref, pad_ref, h_s):
    dh = xc_ref.shape[-1]
    half = dh // 2
    nv = ROW_TILE // SUBLANES
    sp = _softplus(-lam_ref[...])
    sub = lax.broadcasted_iota(jnp.int32, (SUBLANES, dh), 0)

    nlb = dh // LANES

    def gates(i, dirn):
        base = i * ROW_TILE + SUBLANES - CONV_PAD
        vr = [jnp.concatenate([pad_ref[lb, pl.ds(base + j, SUBLANES, stride=nv), :] for lb in range(nlb)], axis=1)
              for j in range(nv + CONV_W - 1)]
        xc = cb_ref[...] + cw_ref[0:1, :] * jnp.concatenate(vr[0:nv], axis=0)
        for tap in range(1, CONV_W):
            xc = xc + cw_ref[tap:tap + 1, :] * jnp.concatenate(vr[tap:tap + nv], axis=0)
        pre = [_dot(xc[:, hf * half:(hf + 1) * half], wg_ref[hf, :, dirn * dh:(dirn + 1) * dh])
               for hf in range(2)]
        r_pre = jnp.concatenate([pre[0][:, :half], pre[1][:, :half]], axis=1)
        i_pre = jnp.concatenate([pre[0][:, half:], pre[1][:, half:]], axis=1)
        r = _sigmoid(r_pre + br_ref[dirn:dirn + 1, :])
        ig = _sigmoid(i_pre + bi_ref[dirn:dirn + 1, :])
        log_a = -RG_C * r * sp[dirn:dirn + 1, :]
        a = jnp.exp(log_a)
        u = jnp.sqrt(-jnp.tanh(log_a) * (a * a + 1.0)) * (ig * xc)
        return a, u

    def tile_scan(a, u, carry, rev):
        order = list(range(nv))[::-1] if rev else list(range(nv))
        a_cum, h_loc = [None] * nv, [None] * nv
        prev = None
        for j in order:
            aj, uj = a[j * SUBLANES:(j + 1) * SUBLANES, :], u[j * SUBLANES:(j + 1) * SUBLANES, :]
            if prev is None:
                a_cum[j], h_loc[j] = aj, uj
            else:
                a_cum[j], h_loc[j] = aj * a_cum[prev], aj * h_loc[prev] + uj
            prev = j
        p, r = a_cum[prev], h_loc[prev]
        for sft in (1, 2, 4):
            keep = (sub < SUBLANES - sft) if rev else (sub >= sft)
            amt = SUBLANES - sft if rev else sft
            r = r + p * jnp.where(keep, pltpu.roll(r, amt, 0), 0.0)
            p = p * jnp.where(keep, pltpu.roll(p, amt, 0), 1.0)
        state = p * carry + r
        keep = (sub < SUBLANES - 1) if rev else (sub >= 1)
        c_in = jnp.where(keep, pltpu.roll(state, SUBLANES - 1 if rev else 1, 0), carry)
        last = 0 if rev else SUBLANES - 1
        return [h_loc[j] + a_cum[j] * c_in for j in range(nv)], state[last:last + 1, :]

    def seg(x_ref, g_ref, o_ref, t_len, h0):
        n_tiles = t_len // ROW_TILE
        zeros = jnp.zeros((SUBLANES, LANES), F32)
        for lb in range(nlb):
            pad_ref[lb, 0:SUBLANES, :] = zeros
            pad_ref[lb, SUBLANES:SUBLANES + t_len, :] = x_ref[:, lb * LANES:(lb + 1) * LANES]
            pad_ref[lb, SUBLANES + t_len:2 * SUBLANES + t_len, :] = zeros
        carries = []
        for dirn in range(2):
            def tile_body(it, carry, dirn=dirn):
                i = it if dirn == 0 else n_tiles - 1 - it
                a, u = gates(i, dirn)
                hs, carry = tile_scan(a, u, carry, dirn == 1)
                for j in range(nv):
                    for lb in range(nlb):
                        h_s[dirn, lb, pl.ds(i * ROW_TILE + j, SUBLANES, stride=nv), :] = (
                            hs[j][:, lb * LANES:(lb + 1) * LANES])
                return carry

            carries.append(lax.fori_loop(0, n_tiles, tile_body, h0[dirn]))
        for i in range(n_tiles):
            rows = slice(i * ROW_TILE, (i + 1) * ROW_TILE)
            hsum = jnp.concatenate([h_s[0, lb, rows, :] + h_s[1, lb, rows, :] for lb in range(nlb)], axis=1)
            o_ref[rows, :] = hsum * _silu(g_ref[rows, :])
        return tuple(carries)

    zero = jnp.zeros((1, dh), F32)
    states = seg(xc_ref, gc_ref, oc_ref, xc_ref.shape[0], (zero, zero))
    seg(xx_ref, gx_ref, ox_ref, xx_ref.shape[0], states)


def _rglru(zc, zx, cw, cb, wg, br, bi, lam):
    bsz, tc_len, _ = zc.shape
    tx_len = zx.shape[1]
    dh = D_HEADS
    nb = dh // LANES

    def zspec(t_len, col):
        return pl.BlockSpec((None, t_len, dh), lambda b: (b, 0, col // nb))

    def full(a):
        return pl.BlockSpec(a.shape, lambda b: (0,) * a.ndim)

    return pl.pallas_call(
        _rglru_kernel,
        grid=(bsz,),
        in_specs=[zspec(tc_len, COL_XA), zspec(tc_len, COL_GA), zspec(tx_len, COL_XA), zspec(tx_len, COL_GA),
                  full(cw), full(cb), full(wg), full(br), full(bi), full(lam)],
        out_specs=[pl.BlockSpec((None, tc_len, dh), lambda b: (b, 0, 0)),
                   pl.BlockSpec((None, tx_len, dh), lambda b: (b, 0, 0))],
        out_shape=[jax.ShapeDtypeStruct((bsz, tc_len, dh), F32),
                   jax.ShapeDtypeStruct((bsz, tx_len, dh), F32)],
        scratch_shapes=[pltpu.VMEM((nb, tx_len + 2 * SUBLANES, LANES), F32),
                        pltpu.VMEM((2, nb, tx_len, LANES), F32)],
        compiler_params=_params(),
        name="rglru",
    )(zc, zc, zx, zx, cw, cb, wg, br, bi, lam)


GDN_UNROLL = 4


def _gdn_masks():
    c = CHUNK_B
    i = np.arange(c)[:, None]
    k = np.arange(c)[None, :]
    ones = np.ones((c, c), bool)
    stacked = np.stack([np.concatenate([k <= i, k > i, ones], 0), np.concatenate([k >= i, k < i, ones], 0)])
    incl = np.stack([k <= i, k >= i])
    return np.tile(stacked, (1, 1, 3)).astype(np.float32), np.tile(incl, (1, 1, 3)).astype(np.float32)


def _split3_rows(x):
    hi = x.astype(BF16)
    r1 = x - hi.astype(F32)
    mid = r1.astype(BF16)
    lo = (r1 - mid.astype(F32)).astype(BF16)
    return jnp.concatenate([hi, mid, lo], axis=0)


def _pair_lhs(x2):
    hi = x2.astype(BF16)
    lo = (x2 - hi.astype(F32)).astype(BF16)
    return jnp.concatenate([hi, lo], axis=1)


def _pair_rhs(x2):
    hi = x2.astype(BF16)
    lo = (x2 - hi.astype(F32)).astype(BF16)
    return jnp.concatenate([hi, lo, hi, lo], axis=0)


def _gdn_kernel(qc_ref, kc_ref, vc_ref, abc_ref, gtc_ref, qx_ref, kx_ref, vx_ref, abx_ref, gtx_ref,
                cwq_ref, cwk_ref, cwv_ref, prm_ref, nw_ref, mst_ref, incl_ref, oc_ref, ox_ref,
                pad_ref, q_s, k_s, v_s, gf_s, bf_s, gb_s, bb_s, o_s, qp_s, p_s, n_s, et_s):
    head = pl.program_id(1)
    c = CHUNK_B
    ii = lax.broadcasted_iota(jnp.int32, (c, LANES), 0)
    jd = lax.broadcasted_iota(jnp.int32, (c, LANES), 1) % c
    eye2 = (ii == jd).astype(F32)
    a_mask2 = (ii > jd, ii < jd)
    strk2 = (a_mask2[0].astype(F32), a_mask2[1].astype(F32))
    s_mask = ((ii >= jd)[:, :c], (ii <= jd)[:, :c])
    sel_r = lax.broadcasted_iota(jnp.int32, (3 * LANES, 4 * LANES), 0) % LANES
    sel_b = lax.broadcasted_iota(jnp.int32, (3 * LANES, 4 * LANES), 1) // LANES
    sel4 = (sel_r == head + H_B * (2 * (sel_b % 2) + sel_b // 2)).astype(BF16)
    lane = lax.broadcasted_iota(jnp.int32, (ROW_TILE, LANES), 1)

    def prep(q_ref, k_ref, v_ref, ab_ref, t_len):
        n_tiles = t_len // ROW_TILE
        for src, dst, cw_ref, kind in ((q_ref, q_s, cwq_ref, "q"), (k_ref, k_s, cwk_ref, "k"),
                                       (v_ref, v_s, cwv_ref, "v")):
            _fill_pad(pad_ref, src, t_len)
            for i in range(n_tiles):
                xc = _silu(_conv_tile(pad_ref, cw_ref, i))
                if kind != "v":
                    xc = xc * lax.rsqrt(jnp.sum(xc * xc, axis=-1, keepdims=True) + EPS)
                if kind == "q":
                    xc = xc * (DK_B ** -0.5)
                dst[i * ROW_TILE:(i + 1) * ROW_TILE, :] = xc
        for i in range(n_tiles):
            rows = slice(i * ROW_TILE, (i + 1) * ROW_TILE)
            ab = ab_ref[rows, :]
            gval = -jnp.exp(prm_ref[0:1, :]) * _softplus(ab + prm_ref[1:2, :])
            gbv = jnp.where(lane < 2 * H_B, gval, _sigmoid(ab))
            hi = gbv.astype(BF16)
            r1 = gbv - hi.astype(F32)
            mid = r1.astype(BF16)
            lo = (r1 - mid.astype(F32)).astype(BF16)
            bc = jnp.dot(jnp.concatenate([hi, mid, lo], axis=1), sel4, preferred_element_type=F32)
            for j, dst in enumerate((gf_s, bf_s, gb_s, bb_s)):
                dst[rows, :] = bc[:, j * LANES:(j + 1) * LANES]

    def chunk_terms(items, hooks=()):
        hooks = list(hooks)

        def splice():
            if hooks:
                hooks.pop(0)()

        m = range(len(items))
        dirs = [d for d, _ in items]
        rows = [pl.ds(pl.multiple_of(ci * c, c), c) for _, ci in items]
        prow = [pl.ds(pl.multiple_of(ci * LANES, LANES), LANES) for _, ci in items]
        erow = [pl.ds(pl.multiple_of(ci * SUBLANES, SUBLANES), SUBLANES) for _, ci in items]
        q = [q_s[r, :] for r in rows]
        k = [k_s[r, :] for r in rows]
        v = [v_s[r, :] for r in rows]
        g = [(gf_s, gb_s)[d][r, :] for d, r in zip(dirs, rows)]
        beta = [(bf_s, bb_s)[d][r, :] for d, r in zip(dirs, rows)]
        cum = [jnp.dot(mst_ref[dirs[i]], _split3_rows(g[i]), preferred_element_type=F32) for i in m]
        dlt = [jnp.dot(incl_ref[dirs[i]], _split3_rows(g[i] * strk2[dirs[i]]), preferred_element_type=F32)
               for i in m]
        kb = [k[i] * beta[i] for i in m]
        kq = [_dot_nt(jnp.concatenate([kb[i], q[i]], axis=0), jnp.concatenate([k[i], k[i]], axis=0)) for i in m]
        splice()
        kk2 = [kq[i][:c] for i in m]
        qk0 = [kq[i][c:, :c] for i in m]
        e_in = [jnp.exp(cum[i][0:c]) for i in m]
        e_rest = [jnp.exp(cum[i][c:2 * c]) for i in m]
        dec2 = [jnp.exp(dlt[i]) for i in m]
        for i in m:
            et_s[dirs[i], erow[i], :] = jnp.exp(cum[i][2 * c:2 * c + SUBLANES])
        a2 = [jnp.where(a_mask2[dirs[i]], kk2[i] * dec2[i], 0.0) for i in m]
        qk = [jnp.where(s_mask[dirs[i]], qk0[i] * dec2[i][:, :c], 0.0) for i in m]
        qsum = [_dot3(_pair_lhs(a2[i]), _pair_rhs(a2[i])) for i in m]
        xp = [_dot3(_pair_lhs(qsum[i]), _pair_rhs(qsum[i])) for i in m]
        splice()
        for lvl in range(3):
            st = [_dot3(jnp.concatenate([_pair_lhs(qsum[i]), _pair_lhs(xp[i])], axis=0), _pair_rhs(xp[i]))
                  for i in m]
            qsum = [qsum[i] + xp[i] + st[i][:c] for i in m]
            xp = [st[i][c:] for i in m]
            if lvl == 1:
                splice()
        qsum = [qsum[i] + xp[i] + _dot3(_pair_lhs(qsum[i]), _pair_rhs(xp[i])) for i in m]
        splice()
        ima = [eye2 - a2[i] for i in m]
        t_inv = [(ima[i] + _dot3(_pair_lhs(ima[i]), _pair_rhs(qsum[i])))[:, :c] for i in m]
        uw = [_dot(t_inv[i], jnp.concatenate([v[i] * beta[i], kb[i] * e_in[i]], axis=1)) for i in m]
        res = [_dot(jnp.concatenate([qk[i], (k[i] * e_rest[i]).T], axis=0), uw[i]) for i in m]
        for i in m:
            n_s[dirs[i], prow[i], :] = res[i][c:, :LANES]
            p_s[dirs[i], prow[i], :] = res[i][c:, LANES:].astype(BF16)
            o_s[dirs[i], rows[i], :] = res[i][:c, :LANES]
            qp_s[dirs[i], rows[i], :] = (q[i] * e_in[i] - res[i][:c, LANES:]).astype(BF16)

    def seg(q_ref, k_ref, v_ref, ab_ref, gt_ref, o_ref, states):
        t_len = q_ref.shape[0]
        n_chunks = t_len // c
        prep(q_ref, k_ref, v_ref, ab_ref, t_len)

        n_pass = n_chunks // GDN_UNROLL

        def terms(p, hooks=()):
            chunk_terms([(dirn, p * GDN_UNROLL + j if dirn == 0 else n_chunks - 1 - (p * GDN_UNROLL + j))
                         for j in range(GDN_UNROLL) for dirn in range(2)], hooks)

        def state_step(n, st):
            new = []
            for dirn, ci in ((0, n), (1, n_chunks - 1 - n)):
                rows = pl.ds(pl.multiple_of(ci * c, c), c)
                prow = pl.ds(pl.multiple_of(ci * LANES, LANES), LANES)
                sb = st[dirn].astype(BF16)
                o_s[dirn, rows, :] = o_s[dirn, rows, :] + jnp.dot(qp_s[dirn, rows, :], sb,
                                                                 preferred_element_type=F32)
                e_tot = et_s[dirn, pl.ds(pl.multiple_of(ci * SUBLANES, SUBLANES), 1), :]
                new.append(st[dirn] * e_tot + n_s[dirn, prow, :]
                           - jnp.dot(p_s[dirn, prow, :], sb, preferred_element_type=F32))
            return tuple(new)

        terms(0)

        def pass_body(p, st):
            holder = [st]

            def hook(j):
                def run():
                    holder[0] = state_step((p - 1) * GDN_UNROLL + j, holder[0])
                return run

            terms(p, [hook(j) for j in range(GDN_UNROLL)])
            return holder[0]

        states = lax.fori_loop(1, n_pass, pass_body, states)
        for j in range(GDN_UNROLL):
            states = state_step((n_pass - 1) * GDN_UNROLL + j, states)
        for i in range(t_len // ROW_TILE):
            rows = slice(i * ROW_TILE, (i + 1) * ROW_TILE)
            o = o_s[0, rows, :] + o_s[1, rows, :]
            o_ref[rows, :] = _rms(o) * nw_ref[...] * _silu(gt_ref[rows, :])
        return states

    zero = jnp.zeros((LANES, LANES), F32)
    states = seg(qc_ref, kc_ref, vc_ref, abc_ref, gtc_ref, oc_ref, (zero, zero))
    seg(qx_ref, kx_ref, vx_ref, abx_ref, gtx_ref, ox_ref, states)


def _gdn(zc, zx, cwb, prm, nw):
    bsz, tc_len, _ = zc.shape
    tx_len = zx.shape[1]

    def zspec(t_len, col, per_head=True):
        if per_head:
            return pl.BlockSpec((None, t_len, LANES), lambda b, h: (b, 0, col + h))
        return pl.BlockSpec((None, t_len, LANES), lambda b, h: (b, 0, col))

    def seg_specs(t_len):
        return [zspec(t_len, COL_Q), zspec(t_len, COL_K), zspec(t_len, COL_V),
                zspec(t_len, COL_AB, False), zspec(t_len, COL_GB)]

    def cw_spec(off):
        return pl.BlockSpec((CONV_W, LANES), lambda b, h: (0, off + h))

    mst_np, incl_np = _gdn_masks()
    mst, incl = jnp.asarray(mst_np, BF16), jnp.asarray(incl_np, BF16)
    n_chunks = tx_len // CHUNK_B
    return pl.pallas_call(
        _gdn_kernel,
        grid=(bsz, H_B),
        in_specs=seg_specs(tc_len) + seg_specs(tx_len) + [
            cw_spec(0), cw_spec(H_B), cw_spec(2 * H_B),
            pl.BlockSpec(prm.shape, lambda b, h: (0, 0)),
            pl.BlockSpec(nw.shape, lambda b, h: (0, 0)),
            pl.BlockSpec(mst.shape, lambda b, h: (0, 0, 0)),
            pl.BlockSpec(incl.shape, lambda b, h: (0, 0, 0))],
        out_specs=[pl.BlockSpec((None, tc_len, LANES), lambda b, h: (b, 0, h)),
                   pl.BlockSpec((None, tx_len, LANES), lambda b, h: (b, 0, h))],
        out_shape=[jax.ShapeDtypeStruct((bsz, tc_len, D_HEADS), F32),
                   jax.ShapeDtypeStruct((bsz, tx_len, D_HEADS), F32)],
        scratch_shapes=[pltpu.VMEM((tx_len + 2 * SUBLANES, LANES), F32)]
                       + [pltpu.VMEM((tx_len, LANES), F32) for _ in range(7)]
                       + [pltpu.VMEM((2, tx_len, LANES), F32),
                          pltpu.VMEM((2, tx_len, LANES), BF16),
                          pltpu.VMEM((2, n_chunks * LANES, LANES), BF16),
                          pltpu.VMEM((2, n_chunks * LANES, LANES), F32),
                          pltpu.VMEM((2, n_chunks * SUBLANES, LANES), F32)],
        compiler_params=_params(),
        name="gdn",
    )(zc, zc, zc, zc, zc, zx, zx, zx, zx, zx, cwb, cwb, cwb, prm, nw, mst, incl)


_LEVELS = (1, 2, 4, 8, 16, 32, 64)


def _gla_masks():
    n = TILE_C
    i = np.arange(n)[:, None]
    k = np.arange(n)[None, :]
    sums, scores = [], []
    for lv in _LEVELS:
        same = (i // (2 * lv)) == (k // (2 * lv))
        hi_i, hi_k = (i // lv) % 2, (k // lv) % 2
        sums.append(same & np.where(hi_i == 1, (hi_k == 1) & (k <= i), (hi_k == 0) & (k > i)))
        scores.append(same & (hi_i == 1) & (hi_k == 0))
    sums += [k <= i, k > i]
    scores.append(i == k)
    sums_f = np.stack(sums).astype(np.float32)
    scores_f = np.stack(scores).astype(np.float32)
    sums_all = np.stack([sums_f, sums_f[:, ::-1, ::-1]]).reshape(2, len(sums) * n, n)
    scores_all = np.stack([scores_f, scores_f[:, ::-1, ::-1]])
    return np.tile(sums_all, (1, 1, 2)), scores_all


GLA_UNROLL = 2


def _gla_kernel(qc_ref, ffc_ref, fbc_ref, ic_ref, gtc_ref, qx_ref, ffx_ref, fbx_ref, ix_ref, gtx_ref,
                lb_ref, nw_ref, msum_ref, mscore_ref, oc_ref, ox_ref, oi_s, qd_s, u_s, et_s, *, layer):
    n = TILE_C
    n_lv = len(_LEVELS)
    depth = lb_ref.shape[0] // 2
    lbs = []
    for dirn in range(2):
        rows = [lb_ref[2 * j + dirn:2 * j + dirn + 1, :] for j in range(depth)]
        mx = functools.reduce(jnp.maximum, rows)
        ex = [jnp.exp(r - mx) for r in rows]
        den = functools.reduce(lambda p, q: p + q, ex)
        acc = jnp.zeros_like(mx)
        for j in range(1, layer + 1):
            acc = acc + ex[j] / den
        lbs.append(acc)

    def tile_terms(items, q_ref, f_refs, i_ref):
        m = range(len(items))
        dirs = [d for d, _ in items]
        rows = [pl.ds(pl.multiple_of(ti * n, n), n) for _, ti in items]
        q = [_silu(q_ref[r, :]) for r in rows]
        v = [i_ref[r, :] for r in rows]
        fg = [lbs[dirs[i]] + (1.0 - lbs[dirs[i]]) * _sigmoid(f_refs[dirs[i]][rows[i], :]) for i in m]
        k = [1.0 - fg[i] for i in m]
        lf = [jnp.log(fg[i]) for i in m]
        hi = [lf[i].astype(BF16) for i in m]
        lf2 = [jnp.concatenate([hi[i], (lf[i] - hi[i].astype(F32)).astype(BF16)], axis=0) for i in m]

        sums = [jnp.dot(msum_ref[dirs[i]], lf2[i], preferred_element_type=F32) for i in m]

        def csum(j):
            return [sums[i][j * n:(j + 1) * n, :] for i in m]

        scores = [mscore_ref[dirs[i], n_lv] * _dot_nt(q[i], k[i]) for i in m]
        for j in range(n_lv):
            e = [jnp.exp(x) for x in csum(j)]
            p = [_dot_nt(q[i] * e[i], k[i] * e[i]) for i in m]
            scores = [scores[i] + mscore_ref[dirs[i], j] * p[i] for i in m]
        c_in, c_rest = csum(n_lv), csum(n_lv + 1)
        o_intra = [_dot(scores[i], v[i]) for i in m]
        u = [_dot_tn(v[i], k[i] * jnp.exp(c_rest[i])) for i in m]
        for i, (_, ti) in enumerate(items):
            oi_s[dirs[i], rows[i], :] = o_intra[i]
            qd_s[dirs[i], rows[i], :] = (q[i] * jnp.exp(c_in[i])).astype(BF16)
            u_s[dirs[i], rows[i], :] = u[i]
            et_s[dirs[i], pl.ds(pl.multiple_of(ti * SUBLANES, SUBLANES), SUBLANES), :] = jnp.exp(
                (c_in[i] + c_rest[i])[0:SUBLANES, :])

    def seg(q_ref, ff_ref, fb_ref, i_ref, gt_ref, o_ref, states):
        t_len = q_ref.shape[0]
        n_tiles = t_len // n

        n_pass = n_tiles // GLA_UNROLL

        def terms(p):
            tile_terms([(dirn, p * GLA_UNROLL + j if dirn == 0 else n_tiles - 1 - (p * GLA_UNROLL + j))
                        for j in range(GLA_UNROLL) for dirn in range(2)], q_ref, (ff_ref, fb_ref), i_ref)

        def state_steps(p, st):
            for j in range(GLA_UNROLL):
                ti = p * GLA_UNROLL + j
                new = []
                for dirn, tj in ((0, ti), (1, n_tiles - 1 - ti)):
                    rows = pl.ds(pl.multiple_of(tj * n, n), n)
                    oi_s[dirn, rows, :] = oi_s[dirn, rows, :] + lax.dot_general(
                        qd_s[dirn, rows, :], st[dirn].astype(BF16), (((1,), (1,)), ((), ())),
                        preferred_element_type=F32)
                    e_tot = et_s[dirn, pl.ds(pl.multiple_of(tj * SUBLANES, SUBLANES), 1), :]
                    new.append(st[dirn] * e_tot + u_s[dirn, rows, :])
                st = tuple(new)
            return st

        terms(0)

        def pass_body(p, st):
            st = state_steps(p - 1, st)
            terms(p)
            return st

        states = lax.fori_loop(1, n_pass, pass_body, states)
        states = state_steps(n_pass - 1, states)
        for i in range(t_len // ROW_TILE):
            rows = slice(i * ROW_TILE, (i + 1) * ROW_TILE)
            o = oi_s[0, rows, :] + oi_s[1, rows, :]
            o_ref[rows, :] = _rms(o) * nw_ref[...] * _silu(gt_ref[rows, :])
        return states

    zero = jnp.zeros((LANES, LANES), F32)
    states = seg(qc_ref, ffc_ref, fbc_ref, ic_ref, gtc_ref, oc_ref, (zero, zero))
    seg(qx_ref, ffx_ref, fbx_ref, ix_ref, gtx_ref, ox_ref, states)


def _gla(zc, zx, lb, nw, msum, mscore, layer):
    bsz, tc_len, _ = zc.shape
    tx_len = zx.shape[1]

    def zspec(t_len, col):
        return pl.BlockSpec((None, t_len, LANES), lambda b, h: (b, 0, col + h))

    def seg_specs(t_len):
        return [zspec(t_len, COL_QC), zspec(t_len, COL_FF), zspec(t_len, COL_FB),
                zspec(t_len, COL_IC), zspec(t_len, COL_GC)]

    return pl.pallas_call(
        functools.partial(_gla_kernel, layer=layer),
        grid=(bsz, H_C),
        in_specs=seg_specs(tc_len) + seg_specs(tx_len) + [
            pl.BlockSpec((lb.shape[0], LANES), lambda b, h: (0, h)),
            pl.BlockSpec(nw.shape, lambda b, h: (0, 0)),
            pl.BlockSpec(msum.shape, lambda b, h: (0, 0, 0)),
            pl.BlockSpec(mscore.shape, lambda b, h: (0, 0, 0, 0))],
        out_specs=[pl.BlockSpec((None, tc_len, LANES), lambda b, h: (b, 0, h)),
                   pl.BlockSpec((None, tx_len, LANES), lambda b, h: (b, 0, h))],
        out_shape=[jax.ShapeDtypeStruct((bsz, tc_len, D_HEADS), F32),
                   jax.ShapeDtypeStruct((bsz, tx_len, D_HEADS), F32)],
        scratch_shapes=[pltpu.VMEM((2, tx_len, LANES), F32),
                        pltpu.VMEM((2, tx_len, LANES), BF16),
                        pltpu.VMEM((2, tx_len, LANES), F32),
                        pltpu.VMEM((2, (tx_len // TILE_C) * SUBLANES, LANES), F32)],
        compiler_params=_params(),
        name="hgrn2",
    )(zc, zc, zc, zc, zc, zx, zx, zx, zx, zx, lb, nw, msum, mscore)


def _block_diag_halves(w):
    w4 = w.reshape(2, H_A // 2, HD_A, HD_A)
    eye = jnp.eye(H_A // 2, dtype=w.dtype)
    return jnp.einsum("ghij,hk->ghikj", w4, eye).reshape(2, (H_A // 2) * HD_A, (H_A // 2) * HD_A)


def kernel(x, c, ctx, c_ctx, w_ada, b_ada, norm_pre, norm_post, w_in, conv_a_w, conv_a_b, rg_w_r, rg_b_r,
           rg_w_i, rg_b_i, rg_lam, conv_b_w, gdn_a_log, gdn_dt_bias, gdn_norm, hg_lb, hg_norm, w_out):
    bsz, t_len, d = x.shape
    depth = w_ada.shape[0]
    dh = D_HEADS

    cs = jnp.concatenate([c, c_ctx[None, :], jnp.zeros((16 - bsz - 1, d), F32)], axis=0)
    mod = _ada(cs, w_ada, b_ada)

    qkv_end = 2 * dh + 3 * dh
    w_in_b = w_in.astype(BF16)
    w_in_p = jnp.concatenate(
        [w_in_b[:, :, :qkv_end], w_in_b[:, :, qkv_end + 4 * H_B:], w_in_b[:, :, qkv_end:qkv_end + 4 * H_B],
         jnp.zeros((depth, d, LANES - 4 * H_B), BF16)], axis=-1)
    w_out_b = w_out.astype(BF16)
    msum_np, mscore_np = _gla_masks()
    msum = jnp.asarray(msum_np, BF16)
    mscore = jnp.asarray(mscore_np, F32)
    lb2 = hg_lb.reshape(depth * 2, dh)

    h, hc = x, ctx
    for l in range(depth):
        colmajor = l % 2 == 1
        mx = mod[l, :bsz].reshape(bsz, 1, 3 * d)
        mc = jnp.broadcast_to(mod[l, bsz].reshape(1, 1, 3 * d), (bsz, 1, 3 * d))
        npre = norm_pre[l].reshape(1, d)
        npost = norm_post[l].reshape(1, d)
        zc = _inproj(hc, mc[..., :d], mc[..., d:2 * d], npre, w_in_p[l], colmajor=False)
        zx = _inproj(h, mx[..., :d], mx[..., d:2 * d], npre, w_in_p[l], colmajor=colmajor)

        wg = jnp.concatenate([_block_diag_halves(rg_w_r[l, 0]), _block_diag_halves(rg_w_i[l, 0]),
                              _block_diag_halves(rg_w_r[l, 1]), _block_diag_halves(rg_w_i[l, 1])],
                             axis=-1).astype(BF16)
        ya_c, ya_x = _rglru(zc, zx, conv_a_w[l], conv_a_b[l].reshape(1, dh), wg,
                            rg_b_r[l], rg_b_i[l], rg_lam[l])

        prm = jnp.zeros((SUBLANES, LANES), F32)
        prm = prm.at[0, :2 * H_B].set(gdn_a_log[l].reshape(-1)).at[1, :2 * H_B].set(gdn_dt_bias[l].reshape(-1))
        yb_c, yb_x = _gdn(zc, zx, conv_b_w[l], prm, gdn_norm[l].reshape(1, LANES))

        yc_c, yc_x = _gla(zc, zx, lb2, hg_norm[l].reshape(1, LANES), msum, mscore, l)

        h = _outproj(ya_x, yb_x, yc_x, w_out_b[l], h, mx[..., 2 * d:], npost, colmajor=colmajor)
        if l < depth - 1:
            hc = _outproj(ya_c, yb_c, yc_c, w_out_b[l], hc, mc[..., 2 * d:], npost, colmajor=False)
    return h
```

```python
import functools

import numpy as np
import jax
import jax.numpy as jnp
from jax import lax
from jax.experimental import pallas as pl
from jax.experimental.pallas import tpu as pltpu

F32 = jnp.float32
BF16 = jnp.bfloat16
HI = lax.Precision.HIGHEST

GRID_W = 64
H_A, HD_A = 8, 64
H_B, DK_B = 4, 128
H_C, DK_C = 4, 128
D_HEADS = 512
CONV_W = 4
CONV_PAD = 2
RG_C = 8.0
EPS = 1e-6

LANES = 128
SUBLANES = 8
ROW_TILE = 256
CHUNK_B = 64
TILE_C = 128
VMEM_LIMIT = 56 * 1024 * 1024

COL_XA, COL_GA, COL_Q, COL_K, COL_V, COL_GB = 0, 4, 8, 12, 16, 20
COL_QC, COL_FF, COL_FB, COL_IC, COL_GC, COL_AB = 24, 28, 32, 36, 40, 44
N_PROJ = 45 * LANES


def _dot(a, b):
    return jnp.dot(a.astype(BF16), b.astype(BF16), preferred_element_type=F32)


def _dot_nt(a, b):
    return lax.dot_general(a.astype(BF16), b.astype(BF16), (((1,), (1,)), ((), ())),
                           preferred_element_type=F32)


def _dot_tn(a, b):
    return lax.dot_general(a.astype(BF16), b.astype(BF16), (((0,), (0,)), ((), ())),
                           preferred_element_type=F32)


def _dot_hi(a, b):
    return jnp.dot(a, b, precision=HI, preferred_element_type=F32)


def _sigmoid(x):
    return 1.0 / (1.0 + jnp.exp(-x))


def _silu(x):
    return x * _sigmoid(x)


def _softplus(x):
    return jnp.maximum(x, 0.0) + jnp.log1p(jnp.exp(-jnp.abs(x)))


def _rms(x):
    return x * lax.rsqrt(jnp.mean(x * x, axis=-1, keepdims=True) + EPS)


def _params(**kw):
    return pltpu.CompilerParams(vmem_limit_bytes=VMEM_LIMIT, **kw)


def _ada_kernel(s_ref, w_ref, b_ref, o_ref):
    o_ref[...] = _dot_hi(_silu(s_ref[...]), w_ref[...]) + b_ref[...]


def _ada(cs, w_ada, b_ada):
    depth, d, n3 = w_ada.shape
    tn = 1024
    return pl.pallas_call(
        _ada_kernel,
        grid=(depth, n3 // tn),
        in_specs=[pl.BlockSpec((16, d), lambda l, j: (0, 0)),
                  pl.BlockSpec((None, d, tn), lambda l, j: (l, 0, j)),
                  pl.BlockSpec((None, 1, tn), lambda l, j: (l, 0, j))],
        out_specs=pl.BlockSpec((None, 16, tn), lambda l, j: (l, 0, j)),
        out_shape=jax.ShapeDtypeStruct((depth, 16, n3), F32),
        compiler_params=_params(),
        name="ada",
    )(cs, w_ada, b_ada.reshape(depth, 1, n3))


def _inproj_kernel(h_ref, sh_ref, sc_ref, npre_ref, w_ref, z_ref, *, colmajor, cn):
    if colmajor:
        x = jnp.concatenate([h_ref[:, j, :] for j in range(h_ref.shape[1])], axis=0)
    else:
        x = h_ref[...]
    u = (_rms(x) * npre_ref[...] * (1.0 + sc_ref[...]) + sh_ref[...]).astype(BF16)
    for j in range(z_ref.shape[-1] // cn):
        z_ref[:, j * cn:(j + 1) * cn] = jnp.dot(u, w_ref[:, j * cn:(j + 1) * cn],
                                               preferred_element_type=F32)


def _inproj(h, shift, scale, npre, w, *, colmajor):
    bsz, t_len, d = h.shape
    n = w.shape[-1]
    tc = ROW_TILE
    if colmajor:
        rows = t_len // GRID_W
        h_in = h.reshape(bsz, rows, GRID_W, d)
        h_spec = pl.BlockSpec((None, rows, tc // rows, d), lambda b, i: (b, 0, i, 0))
    else:
        h_in = h
        h_spec = pl.BlockSpec((None, tc, d), lambda b, i: (b, i, 0))
    vec = pl.BlockSpec((None, 1, d), lambda b, i: (b, 0, 0))
    return pl.pallas_call(
        functools.partial(_inproj_kernel, colmajor=colmajor, cn=1152),
        grid=(bsz, t_len // tc),
        in_specs=[h_spec, vec, vec,
                  pl.BlockSpec((1, d), lambda b, i: (0, 0)),
                  pl.BlockSpec((d, n), lambda b, i: (0, 0))],
        out_specs=pl.BlockSpec((None, tc, n), lambda b, i: (b, i, 0)),
        out_shape=jax.ShapeDtypeStruct((bsz, t_len, n), F32),
        compiler_params=_params(),
        name="inproj",
    )(h_in, shift, scale, npre, w)


def _outproj_kernel(ya_ref, yb_ref, yc_ref, w_ref, h_ref, gt_ref, npost_ref, o_ref, *, colmajor):
    dm = ya_ref.shape[-1]
    d = npost_ref.shape[-1]
    o = (jnp.dot(ya_ref[...].astype(BF16), w_ref[0:dm, :], preferred_element_type=F32)
         + jnp.dot(yb_ref[...].astype(BF16), w_ref[dm:2 * dm, :], preferred_element_type=F32)
         + jnp.dot(yc_ref[...].astype(BF16), w_ref[2 * dm:3 * dm, :], preferred_element_type=F32))
    upd = gt_ref[...] * (_rms(o) * npost_ref[...])
    if colmajor:
        rows = h_ref.shape[0]
        for j in range(h_ref.shape[1]):
            o_ref[:, j, :] = h_ref[:, j, :] + upd[j * rows:(j + 1) * rows, :]
    else:
        o_ref[...] = h_ref[...] + upd


def _outproj(ya, yb, yc, w, h, gate, npost, *, colmajor):
    bsz, t_len, d = h.shape
    dm = ya.shape[-1]
    tc = ROW_TILE
    if colmajor:
        rows = t_len // GRID_W
        h_in = h.reshape(bsz, rows, GRID_W, d)
        h_spec = pl.BlockSpec((None, rows, tc // rows, d), lambda b, i: (b, 0, i, 0))
    else:
        h_in = h
        h_spec = pl.BlockSpec((None, tc, d), lambda b, i: (b, i, 0))
    y_spec = pl.BlockSpec((None, tc, dm), lambda b, i: (b, i, 0))
    out = pl.pallas_call(
        functools.partial(_outproj_kernel, colmajor=colmajor),
        grid=(bsz, t_len // tc),
        in_specs=[y_spec, y_spec, y_spec,
                  pl.BlockSpec((3 * dm, d), lambda b, i: (0, 0)),
                  h_spec,
                  pl.BlockSpec((None, 1, d), lambda b, i: (b, 0, 0)),
                  pl.BlockSpec((1, d), lambda b, i: (0, 0))],
        out_specs=h_spec,
        out_shape=jax.ShapeDtypeStruct(h_in.shape, F32),
        compiler_params=_params(),
        name="outproj",
    )(ya, yb, yc, w, h_in, gate, npost)
    return out.reshape(bsz, t_len, d)


def _conv_tile(pad_ref, cw_ref, i):
    base = i * ROW_TILE + SUBLANES - CONV_PAD
    acc = cw_ref[0:1, :] * pad_ref[base:base + ROW_TILE, :]
    for j in range(1, CONV_W):
        acc = acc + cw_ref[j:j + 1, :] * pad_ref[base + j:base + j + ROW_TILE, :]
    return acc


def _fill_pad(pad_ref, x_ref, t_len):
    zeros = jnp.zeros((SUBLANES, pad_ref.shape[-1]), F32)
    pad_ref[0:SUBLANES, :] = zeros
    pad_ref[SUBLANES:SUBLANES + t_len, :] = x_ref[...]
    pad_ref[SUBLANES + t_len:2 * SUBLANES + t_len, :] = zeros


def _rglru_kernel(xc_ref, gc_ref, xx_ref, gx_ref, cw_ref, cb_ref, wg_ref, br_ref, bi_ref, lam_ref,
                  oc_ref, ox_ref, pad_ref, h_s):
    dh = xc_ref.shape[-1]
    half = dh // 2
    nv = ROW_TILE // SUBLANES
    sp = _softplus(-lam_ref[...])
    sub = lax.broadcasted_iota(jnp.int32, (SUBLANES, dh), 0)

    nlb = dh // LANES

    def gates(i, dirn):
        base = i * ROW_TILE + SUBLANES - CONV_PAD
        vr = [jnp.concatenate([pad_ref[lb, pl.ds(base + j, SUBLANES, stride=nv), :] for lb in range(nlb)], axis=1)
              for j in range(nv + CONV_W - 1)]
        xc = cb_ref[...] + cw_ref[0:1, :] * jnp.concatenate(vr[0:nv], axis=0)
        for tap in range(1, CONV_W):
            xc = xc + cw_ref[tap:tap + 1, :] * jnp.concatenate(vr[tap:tap + nv], axis=0)
        pre = [_dot(xc[:, hf * half:(hf + 1) * half], wg_ref[hf, :, dirn * dh:(dirn + 1) * dh])
               for hf in range(2)]
        r_pre = jnp.concatenate([pre[0][:, :half], pre[1][:, :half]], axis=1)
        i_pre = jnp.concatenate([pre[0][:, half:], pre[1][:, half:]], axis=1)
        r = _sigmoid(r_pre + br_ref[dirn:dirn + 1, :])
        ig = _sigmoid(i_pre + bi_ref[dirn:dirn + 1, :])
        log_a = -RG_C * r * sp[dirn:dirn + 1, :]
        a = jnp.exp(log_a)
        u = jnp.sqrt(-jnp.tanh(log_a) * (a * a + 1.0)) * (ig * xc)
        return a, u

    def tile_scan(a, u, carry, rev):
        order = list(range(nv))[::-1] if rev else list(range(nv))
        a_cum, h_loc = [None] * nv, [None] * nv
        prev = None
        for j in order:
            aj, uj = a[j * SUBLANES:(j + 1) * SUBLANES, :], u[j * SUBLANES:(j + 1) * SUBLANES, :]
            if prev is None:
                a_cum[j], h_loc[j] = aj, uj
            else:
                a_cum[j], h_loc[j] = aj * a_cum[prev], aj * h_loc[prev] + uj
            prev = j
        p, r = a_cum[prev], h_loc[prev]
        for sft in (1, 2, 4):
            keep = (sub < SUBLANES - sft) if rev else (sub >= sft)
            amt = SUBLANES - sft if rev else sft
            r = r + p * jnp.where(keep, pltpu.roll(r, amt, 0), 0.0)
            p = p * jnp.where(keep, pltpu.roll(p, amt, 0), 1.0)
        state = p * carry + r
        keep = (sub < SUBLANES - 1) if rev else (sub >= 1)
        c_in = jnp.where(keep, pltpu.roll(state, SUBLANES - 1 if rev else 1, 0), carry)
        last = 0 if rev else SUBLANES - 1
        return [h_loc[j] + a_cum[j] * c_in for j in range(nv)], state[last:last + 1, :]

    def seg(x_ref, g_ref, o_ref, t_len, h0):
        n_tiles = t_len // ROW_TILE
        zeros = jnp.zeros((SUBLANES, LANES), F32)
        for lb in range(nlb):
            pad_ref[lb, 0:SUBLANES, :] = zeros
            pad_ref[lb, SUBLANES:SUBLANES + t_len, :] = x_ref[:, lb * LANES:(lb + 1) * LANES]
            pad_ref[lb, SUBLANES + t_len:2 * SUBLANES + t_len, :] = zeros
        carries = []
        for dirn in range(2):
            def tile_body(it, carry, dirn=dirn):
                i = it if dirn == 0 else n_tiles - 1 - it
                a, u = gates(i, dirn)
                hs, carry = tile_scan(a, u, carry, dirn == 1)
                for j in range(nv):
                    for lb in range(nlb):
                        h_s[dirn, lb, pl.ds(i * ROW_TILE + j, SUBLANES, stride=nv), :] = (
                            hs[j][:, lb * LANES:(lb + 1) * LANES])
                return carry

            carries.append(lax.fori_loop(0, n_tiles, tile_body, h0[dirn]))
        for i in range(n_tiles):
            rows = slice(i * ROW_TILE, (i + 1) * ROW_TILE)
            hsum = jnp.concatenate([h_s[0, lb, rows, :] + h_s[1, lb, rows, :] for lb in range(nlb)], axis=1)
            o_ref[rows, :] = hsum * _silu(g_ref[rows, :])
        return tuple(carries)

    zero = jnp.zeros((1, dh), F32)
    states = seg(xc_ref, gc_ref, oc_ref, xc_ref.shape[0], (zero, zero))
    seg(xx_ref, gx_ref, ox_ref, xx_ref.shape[0], states)


def _rglru(zc, zx, cw, cb, wg, br, bi, lam):
    bsz, tc_len, _ = zc.shape
    tx_len = zx.shape[1]
    dh = D_HEADS
    nb = dh // LANES

    def zspec(t_len, col):
        return pl.BlockSpec((None, t_len, dh), lambda b: (b, 0, col // nb))

    def full(a):
        return pl.BlockSpec(a.shape, lambda b: (0,) * a.ndim)

    return pl.pallas_call(
        _rglru_kernel,
        grid=(bsz,),
        in_specs=[zspec(tc_len, COL_XA), zspec(tc_len, COL_GA), zspec(tx_len, COL_XA), zspec(tx_len, COL_GA),
                  full(cw), full(cb), full(wg), full(br), full(bi), full(lam)],
        out_specs=[pl.BlockSpec((None, tc_len, dh), lambda b: (b, 0, 0)),
                   pl.BlockSpec((None, tx_len, dh), lambda b: (b, 0, 0))],
        out_shape=[jax.ShapeDtypeStruct((bsz, tc_len, dh), F32),
                   jax.ShapeDtypeStruct((bsz, tx_len, dh), F32)],
        scratch_shapes=[pltpu.VMEM((nb, tx_len + 2 * SUBLANES, LANES), F32),
                        pltpu.VMEM((2, nb, tx_len, LANES), F32)],
        compiler_params=_params(),
        name="rglru",
    )(zc, zc, zx, zx, cw, cb, wg, br, bi, lam)


GDN_UNROLL = 8
GDN_SPLICES = 4


def _gdn_masks():
    c = CHUNK_B
    i = np.arange(c)[:, None]
    k = np.arange(c)[None, :]
    ones = np.ones((c, c), bool)
    stacked = np.stack([np.concatenate([k <= i, k > i, ones], 0), np.concatenate([k >= i, k < i, ones], 0)])
    incl = np.stack([k <= i, k >= i])
    return np.tile(stacked, (1, 1, 3)).astype(np.float32), np.tile(incl, (1, 1, 3)).astype(np.float32)


def _split3_rows(x):
    hi = x.astype(BF16)
    r1 = x - hi.astype(F32)
    mid = r1.astype(BF16)
    lo = (r1 - mid.astype(F32)).astype(BF16)
    return jnp.concatenate([hi, mid, lo], axis=0)


INV_BLOCK = 16


def _gdn_kernel(qc_ref, kc_ref, vc_ref, abc_ref, gtc_ref, qx_ref, kx_ref, vx_ref, abx_ref, gtx_ref,
                cwq_ref, cwk_ref, cwv_ref, prm_ref, nw_ref, mst_ref, incl_ref, oc_ref, ox_ref,
                pad_ref, q_s, k_s, v_s, gf_s, bf_s, gb_s, bb_s, o_s, qp_s, p_s, n_s, et_s):
    head = pl.program_id(1)
    c = CHUNK_B
    ii = lax.broadcasted_iota(jnp.int32, (c, c), 0)
    jj = lax.broadcasted_iota(jnp.int32, (c, c), 1)
    eye = (ii == jj).astype(F32)
    a_mask = (ii > jj, ii < jj)
    strk = (a_mask[0].astype(F32), a_mask[1].astype(F32))
    s_mask = (ii >= jj, ii <= jj)
    d_mask = (ii // INV_BLOCK) == (jj // INV_BLOCK)
    f_mask = []
    size = INV_BLOCK
    while size < c:
        same = (ii // (2 * size)) == (jj // (2 * size))
        hi_i, hi_j = (ii // size) % 2, (jj // size) % 2
        f_mask.append((same & (hi_i == 1) & (hi_j == 0), same & (hi_i == 0) & (hi_j == 1)))
        size *= 2
    sel_r = lax.broadcasted_iota(jnp.int32, (3 * LANES, 4 * LANES), 0) % LANES
    sel_b = lax.broadcasted_iota(jnp.int32, (3 * LANES, 4 * LANES), 1) // LANES
    sel4 = (sel_r == head + H_B * (2 * (sel_b % 2) + sel_b // 2)).astype(BF16)
    lane = lax.broadcasted_iota(jnp.int32, (ROW_TILE, LANES), 1)

    def prep(q_ref, k_ref, v_ref, ab_ref, t_len):
        n_tiles = t_len // ROW_TILE
        for src, dst, cw_ref, kind in ((q_ref, q_s, cwq_ref, "q"), (k_ref, k_s, cwk_ref, "k"),
                                       (v_ref, v_s, cwv_ref, "v")):
            _fill_pad(pad_ref, src, t_len)
            for i in range(n_tiles):
                xc = _silu(_conv_tile(pad_ref, cw_ref, i))
                if kind != "v":
                    xc = xc * lax.rsqrt(jnp.sum(xc * xc, axis=-1, keepdims=True) + EPS)
                if kind == "q":
                    xc = xc * (DK_B ** -0.5)
                dst[i * ROW_TILE:(i + 1) * ROW_TILE, :] = xc
        for i in range(n_tiles):
            rows = slice(i * ROW_TILE, (i + 1) * ROW_TILE)
            ab = ab_ref[rows, :]
            gval = -jnp.exp(prm_ref[0:1, :]) * _softplus(ab + prm_ref[1:2, :])
            gbv = jnp.where(lane < 2 * H_B, gval, _sigmoid(ab))
            hi = gbv.astype(BF16)
            r1 = gbv - hi.astype(F32)
            mid = r1.astype(BF16)
            lo = (r1 - mid.astype(F32)).astype(BF16)
            bc = jnp.dot(jnp.concatenate([hi, mid, lo], axis=1), sel4, preferred_element_type=F32)
            for j, dst in enumerate((gf_s, bf_s, gb_s, bb_s)):
                dst[rows, :] = bc[:, j * LANES:(j + 1) * LANES]

    def chunk_terms(items, hooks=()):
        hooks = list(hooks)
        per_splice = -(-len(hooks) // GDN_SPLICES)

        def splice():
            for _ in range(min(per_splice, len(hooks))):
                hooks.pop(0)()

        m = range(len(items))
        dirs = [d for d, _ in items]
        rows = [pl.ds(pl.multiple_of(ci * c, c), c) for _, ci in items]
        prow = [pl.ds(pl.multiple_of(ci * LANES, LANES), LANES) for _, ci in items]
        erow = [pl.ds(pl.multiple_of(ci * SUBLANES, SUBLANES), SUBLANES) for _, ci in items]
        q = [q_s[r, :] for r in rows]
        k = [k_s[r, :] for r in rows]
        v = [v_s[r, :] for r in rows]
        g = [(gf_s, gb_s)[d][r, :] for d, r in zip(dirs, rows)]
        beta = [(bf_s, bb_s)[d][r, :] for d, r in zip(dirs, rows)]
        cum = [jnp.dot(mst_ref[dirs[i]], _split3_rows(g[i]), preferred_element_type=F32) for i in m]
        dlt = [jnp.dot(incl_ref[dirs[i]], _split3_rows(g[i][:, :c] * strk[dirs[i]]), preferred_element_type=F32)
               for i in m]
        kb = [k[i] * beta[i] for i in m]
        kq = [_dot_nt(jnp.concatenate([kb[i], q[i]], axis=0), k[i]) for i in m]
        splice()
        e_in = [jnp.exp(cum[i][0:c]) for i in m]
        e_rest = [jnp.exp(cum[i][c:2 * c]) for i in m]
        dec = [jnp.exp(dlt[i]) for i in m]
        for i in m:
            et_s[dirs[i], erow[i], :] = jnp.exp(cum[i][2 * c:2 * c + SUBLANES])
        a = [jnp.where(a_mask[dirs[i]], kq[i][:c] * dec[i], 0.0) for i in m]
        qk = [jnp.where(s_mask[dirs[i]], kq[i][c:] * dec[i], 0.0) for i in m]
        dg = [jnp.where(d_mask, a[i], 0.0) for i in m]
        x1 = [_dot(dg[i], dg[i]) for i in m]
        x2 = [_dot(x1[i], x1[i]) for i in m]
        splice()
        st = [_dot(jnp.concatenate([x1[i], x2[i]], axis=0), x2[i]) for i in m]
        q2 = [x1[i] + x2[i] + st[i][:c] for i in m]
        q3 = [q2[i] + st[i][c:] + _dot(q2[i], st[i][c:]) for i in m]
        imd = [eye - dg[i] for i in m]
        t_inv = [imd[i] + _dot(imd[i], q3[i]) for i in m]
        splice()
        for fm in f_mask:
            tf = [_dot(t_inv[i], jnp.where(fm[dirs[i]], a[i], 0.0)) for i in m]
            t_inv = [t_inv[i] - _dot(tf[i], t_inv[i]) for i in m]
        splice()
        uw = [_dot(t_inv[i], jnp.concatenate([v[i] * beta[i], kb[i] * e_in[i]], axis=1)) for i in m]
        res = [_dot(jnp.concatenate([qk[i], (k[i] * e_rest[i]).T], axis=0), uw[i]) for i in m]
        for i in m:
            n_s[dirs[i], prow[i], :] = res[i][c:, :LANES]
            p_s[dirs[i], prow[i], :] = res[i][c:, LANES:].astype(BF16)
            o_s[dirs[i], rows[i], :] = res[i][:c, :LANES]
            qp_s[dirs[i], rows[i], :] = (q[i] * e_in[i] - res[i][:c, LANES:]).astype(BF16)

    def seg(q_ref, k_ref, v_ref, ab_ref, gt_ref, o_ref, states):
        t_len = q_ref.shape[0]
        n_chunks = t_len // c
        prep(q_ref, k_ref, v_ref, ab_ref, t_len)

        unroll = min(GDN_UNROLL, n_chunks)
        n_pass = n_chunks // unroll

        def terms(p, hooks=()):
            chunk_terms([(dirn, p * unroll + j if dirn == 0 else n_chunks - 1 - (p * unroll + j))
                         for j in range(unroll) for dirn in range(2)], hooks)

        def state_step(n, st):
            new = []
            for dirn, ci in ((0, n), (1, n_chunks - 1 - n)):
                rows = pl.ds(pl.multiple_of(ci * c, c), c)
                prow = pl.ds(pl.multiple_of(ci * LANES, LANES), LANES)
                sb = st[dirn].astype(BF16)
                o_s[dirn, rows, :] = o_s[dirn, rows, :] + jnp.dot(qp_s[dirn, rows, :], sb,
                                                                 preferred_element_type=F32)
                e_tot = et_s[dirn, pl.ds(pl.multiple_of(ci * SUBLANES, SUBLANES), 1), :]
                new.append(st[dirn] * e_tot + n_s[dirn, prow, :]
                           - jnp.dot(p_s[dirn, prow, :], sb, preferred_element_type=F32))
            return tuple(new)

        terms(0)

        def pass_body(p, st):
            holder = [st]

            def hook(j):
                def run():
                    holder[0] = state_step((p - 1) * unroll + j, holder[0])
                return run

            terms(p, [hook(j) for j in range(unroll)])
            return holder[0]

        states = lax.fori_loop(1, n_pass, pass_body, states)
        for j in range(unroll):
            states = state_step((n_pass - 1) * unroll + j, states)
        for i in range(t_len // ROW_TILE):
            rows = slice(i * ROW_TILE, (i + 1) * ROW_TILE)
            o = o_s[0, rows, :] + o_s[1, rows, :]
            o_ref[rows, :] = _rms(o) * nw_ref[...] * _silu(gt_ref[rows, :])
        return states

    zero = jnp.zeros((LANES, LANES), F32)
    states = seg(qc_ref, kc_ref, vc_ref, abc_ref, gtc_ref, oc_ref, (zero, zero))
    seg(qx_ref, kx_ref, vx_ref, abx_ref, gtx_ref, ox_ref, states)


def _gdn(zc, zx, cwb, prm, nw):
    bsz, tc_len, _ = zc.shape
    tx_len = zx.shape[1]

    def zspec(t_len, col, per_head=True):
        if per_head:
            return pl.BlockSpec((None, t_len, LANES), lambda b, h: (b, 0, col + h))
        return pl.BlockSpec((None, t_len, LANES), lambda b, h: (b, 0, col))

    def seg_specs(t_len):
        return [zspec(t_len, COL_Q), zspec(t_len, COL_K), zspec(t_len, COL_V),
                zspec(t_len, COL_AB, False), zspec(t_len, COL_GB)]

    def cw_spec(off):
        return pl.BlockSpec((CONV_W, LANES), lambda b, h: (0, off + h))

    mst_np, incl_np = _gdn_masks()
    mst, incl = jnp.asarray(mst_np, BF16), jnp.asarray(incl_np, BF16)
    n_chunks = tx_len // CHUNK_B
    return pl.pallas_call(
        _gdn_kernel,
        grid=(bsz, H_B),
        in_specs=seg_specs(tc_len) + seg_specs(tx_len) + [
            cw_spec(0), cw_spec(H_B), cw_spec(2 * H_B),
            pl.BlockSpec(prm.shape, lambda b, h: (0, 0)),
            pl.BlockSpec(nw.shape, lambda b, h: (0, 0)),
            pl.BlockSpec(mst.shape, lambda b, h: (0, 0, 0)),
            pl.BlockSpec(incl.shape, lambda b, h: (0, 0, 0))],
        out_specs=[pl.BlockSpec((None, tc_len, LANES), lambda b, h: (b, 0, h)),
                   pl.BlockSpec((None, tx_len, LANES), lambda b, h: (b, 0, h))],
        out_shape=[jax.ShapeDtypeStruct((bsz, tc_len, D_HEADS), F32),
                   jax.ShapeDtypeStruct((bsz, tx_len, D_HEADS), F32)],
        scratch_shapes=[pltpu.VMEM((tx_len + 2 * SUBLANES, LANES), F32)]
                       + [pltpu.VMEM((tx_len, LANES), F32) for _ in range(7)]
                       + [pltpu.VMEM((2, tx_len, LANES), F32),
                          pltpu.VMEM((2, tx_len, LANES), BF16),
                          pltpu.VMEM((2, n_chunks * LANES, LANES), BF16),
                          pltpu.VMEM((2, n_chunks * LANES, LANES), F32),
                          pltpu.VMEM((2, n_chunks * SUBLANES, LANES), F32)],
        compiler_params=_params(),
        name="gdn",
    )(zc, zc, zc, zc, zc, zx, zx, zx, zx, zx, cwb, cwb, cwb, prm, nw, mst, incl)


_LEVELS = (1, 2, 4, 8, 16, 32, 64)


def _gla_masks():
    n = TILE_C
    i = np.arange(n)[:, None]
    k = np.arange(n)[None, :]
    sums, scores = [], []
    for lv in _LEVELS:
        same = (i // (2 * lv)) == (k // (2 * lv))
        hi_i, hi_k = (i // lv) % 2, (k // lv) % 2
        if lv < SUBLANES:
            sums.append(same & np.where(hi_i == 1, (hi_k == 1) & (k <= i), (hi_k == 0) & (k > i)))
        scores.append(same & (hi_i == 1) & (hi_k == 0))
    sums += [k <= i, k > i]
    scores.append(i == k)
    sums_f = np.stack(sums).astype(np.float32)
    scores_f = np.stack(scores).astype(np.float32)
    sums_all = np.stack([sums_f, sums_f[:, ::-1, ::-1]]).reshape(2, len(sums) * n, n)
    scores_all = np.stack([scores_f, scores_f[:, ::-1, ::-1]])
    return np.tile(sums_all, (1, 1, 2)), scores_all


GLA_UNROLL = 2


def _gla_kernel(qc_ref, ffc_ref, fbc_ref, ic_ref, gtc_ref, qx_ref, ffx_ref, fbx_ref, ix_ref, gtx_ref,
                lb_ref, nw_ref, msum_ref, mscore_ref, oc_ref, ox_ref, oi_s, qd_s, u_s, et_s, *, layer):
    n = TILE_C
    n_lv = len(_LEVELS)
    depth = lb_ref.shape[0] // 2
    lbs = []
    for dirn in range(2):
        rows = [lb_ref[2 * j + dirn:2 * j + dirn + 1, :] for j in range(depth)]
        mx = functools.reduce(jnp.maximum, rows)
        ex = [jnp.exp(r - mx) for r in rows]
        den = functools.reduce(lambda p, q: p + q, ex)
        acc = jnp.zeros_like(mx)
        for j in range(1, layer + 1):
            acc = acc + ex[j] / den
        lbs.append(acc)

    def tile_terms(items, q_ref, f_refs, i_ref):
        m = range(len(items))
        dirs = [d for d, _ in items]
        rows = [pl.ds(pl.multiple_of(ti * n, n), n) for _, ti in items]
        q = [_silu(q_ref[r, :]) for r in rows]
        v = [i_ref[r, :] for r in rows]
        fg = [lbs[dirs[i]] + (1.0 - lbs[dirs[i]]) * _sigmoid(f_refs[dirs[i]][rows[i], :]) for i in m]
        k = [1.0 - fg[i] for i in m]
        lf = [jnp.log(fg[i]) for i in m]
        hi = [lf[i].astype(BF16) for i in m]
        lf2 = [jnp.concatenate([hi[i], (lf[i] - hi[i].astype(F32)).astype(BF16)], axis=0) for i in m]

        sums = [jnp.dot(msum_ref[dirs[i]], lf2[i], preferred_element_type=F32) for i in m]

        def csum(j):
            return [sums[i][j * n:(j + 1) * n, :] for i in m]

        n_small = sum(1 for lv in _LEVELS if lv < SUBLANES)
        c_in, c_rest = csum(n_small), csum(n_small + 1)

        def boundary_exponent(i, lv):
            off = lv - 1 if dirs[i] == 0 else lv
            ref = jnp.concatenate(
                [jnp.broadcast_to(c_in[i][blk * 2 * lv + off:blk * 2 * lv + off + 1, :], (2 * lv, LANES))
                 for blk in range(n // (2 * lv))], axis=0)
            return -jnp.abs(c_in[i] - ref)

        scores = [mscore_ref[dirs[i], n_lv] * _dot_nt(q[i], k[i]) for i in m]
        for j, lv in enumerate(_LEVELS):
            xs = csum(j) if lv < SUBLANES else [boundary_exponent(i, lv) for i in m]
            e = [jnp.exp(x) for x in xs]
            p = [_dot_nt(q[i] * e[i], k[i] * e[i]) for i in m]
            scores = [scores[i] + mscore_ref[dirs[i], j] * p[i] for i in m]
        o_intra = [_dot(scores[i], v[i]) for i in m]
        u = [_dot_tn(v[i], k[i] * jnp.exp(c_rest[i])) for i in m]
        for i, (_, ti) in enumerate(items):
            oi_s[dirs[i], rows[i], :] = o_intra[i]
            qd_s[dirs[i], rows[i], :] = (q[i] * jnp.exp(c_in[i])).astype(BF16)
            u_s[dirs[i], rows[i], :] = u[i]
            et_s[dirs[i], pl.ds(pl.multiple_of(ti * SUBLANES, SUBLANES), SUBLANES), :] = jnp.exp(
                (c_in[i] + c_rest[i])[0:SUBLANES, :])

    def seg(q_ref, ff_ref, fb_ref, i_ref, gt_ref, o_ref, states):
        t_len = q_ref.shape[0]
        n_tiles = t_len // n

        n_pass = n_tiles // GLA_UNROLL

        def terms(p):
            tile_terms([(dirn, p * GLA_UNROLL + j if dirn == 0 else n_tiles - 1 - (p * GLA_UNROLL + j))
                        for j in range(GLA_UNROLL) for dirn in range(2)], q_ref, (ff_ref, fb_ref), i_ref)

        def state_steps(p, st):
            for j in range(GLA_UNROLL):
                ti = p * GLA_UNROLL + j
                new = []
                for dirn, tj in ((0, ti), (1, n_tiles - 1 - ti)):
                    rows = pl.ds(pl.multiple_of(tj * n, n), n)
                    oi_s[dirn, rows, :] = oi_s[dirn, rows, :] + lax.dot_general(
                        qd_s[dirn, rows, :], st[dirn].astype(BF16), (((1,), (1,)), ((), ())),
                        preferred_element_type=F32)
                    e_tot = et_s[dirn, pl.ds(pl.multiple_of(tj * SUBLANES, SUBLANES), 1), :]
                    new.append(st[dirn] * e_tot + u_s[dirn, rows, :])
                st = tuple(new)
            return st

        terms(0)

        def pass_body(p, st):
            st = state_steps(p - 1, st)
            terms(p)
            return st

        states = lax.fori_loop(1, n_pass, pass_body, states)
        states = state_steps(n_pass - 1, states)
        for i in range(t_len // ROW_TILE):
            rows = slice(i * ROW_TILE, (i + 1) * ROW_TILE)
            o = oi_s[0, rows, :] + oi_s[1, rows, :]
            o_ref[rows, :] = _rms(o) * nw_ref[...] * _silu(gt_ref[rows, :])
        return states

    zero = jnp.zeros((LANES, LANES), F32)
    states = seg(qc_ref, ffc_ref, fbc_ref, ic_ref, gtc_ref, oc_ref, (zero, zero))
    seg(qx_ref, ffx_ref, fbx_ref, ix_ref, gtx_ref, ox_ref, states)


def _gla(zc, zx, lb, nw, msum, mscore, layer):
    bsz, tc_len, _ = zc.shape
    tx_len = zx.shape[1]

    def zspec(t_len, col):
        return pl.BlockSpec((None, t_len, LANES), lambda b, h: (b, 0, col + h))

    def seg_specs(t_len):
        return [zspec(t_len, COL_QC), zspec(t_len, COL_FF), zspec(t_len, COL_FB),
                zspec(t_len, COL_IC), zspec(t_len, COL_GC)]

    return pl.pallas_call(
        functools.partial(_gla_kernel, layer=layer),
        grid=(bsz, H_C),
        in_specs=seg_specs(tc_len) + seg_specs(tx_len) + [
            pl.BlockSpec((lb.shape[0], LANES), lambda b, h: (0, h)),
            pl.BlockSpec(nw.shape, lambda b, h: (0, 0)),
            pl.BlockSpec(msum.shape, lambda b, h: (0, 0, 0)),
            pl.BlockSpec(mscore.shape, lambda b, h: (0, 0, 0, 0))],
        out_specs=[pl.BlockSpec((None, tc_len, LANES), lambda b, h: (b, 0, h)),
                   pl.BlockSpec((None, tx_len, LANES), lambda b, h: (b, 0, h))],
        out_shape=[jax.ShapeDtypeStruct((bsz, tc_len, D_HEADS), F32),
                   jax.ShapeDtypeStruct((bsz, tx_len, D_HEADS), F32)],
        scratch_shapes=[pltpu.VMEM((2, tx_len, LANES), F32),
                        pltpu.VMEM((2, tx_len, LANES), BF16),
                        pltpu.VMEM((2, tx_len, LANES), F32),
                        pltpu.VMEM((2, (tx_len // TILE_C) * SUBLANES, LANES), F32)],
        compiler_params=_params(),
        name="hgrn2",
    )(zc, zc, zc, zc, zc, zx, zx, zx, zx, zx, lb, nw, msum, mscore)


def _block_diag_halves(w):
    w4 = w.reshape(2, H_A // 2, HD_A, HD_A)
    eye = jnp.eye(H_A // 2, dtype=w.dtype)
    return jnp.einsum("ghij,hk->ghikj", w4, eye).reshape(2, (H_A // 2) * HD_A, (H_A // 2) * HD_A)


def kernel(x, c, ctx, c_ctx, w_ada, b_ada, norm_pre, norm_post, w_in, conv_a_w, conv_a_b, rg_w_r, rg_b_r,
           rg_w_i, rg_b_i, rg_lam, conv_b_w, gdn_a_log, gdn_dt_bias, gdn_norm, hg_lb, hg_norm, w_out):
    bsz, t_len, d = x.shape
    depth = w_ada.shape[0]
    dh = D_HEADS

    cs = jnp.concatenate([c, c_ctx[None, :], jnp.zeros((16 - bsz - 1, d), F32)], axis=0)
    mod = _ada(cs, w_ada, b_ada)

    qkv_end = 2 * dh + 3 * dh
    w_in_b = w_in.astype(BF16)
    w_in_p = jnp.concatenate(
        [w_in_b[:, :, :qkv_end], w_in_b[:, :, qkv_end + 4 * H_B:], w_in_b[:, :, qkv_end:qkv_end + 4 * H_B],
         jnp.zeros((depth, d, LANES - 4 * H_B), BF16)], axis=-1)
    w_out_b = w_out.astype(BF16)
    msum_np, mscore_np = _gla_masks()
    msum = jnp.asarray(msum_np, BF16)
    mscore = jnp.asarray(mscore_np, F32)
    lb2 = hg_lb.reshape(depth * 2, dh)

    h, hc = x, ctx
    for l in range(depth):
        colmajor = l % 2 == 1
        mx = mod[l, :bsz].reshape(bsz, 1, 3 * d)
        mc = jnp.broadcast_to(mod[l, bsz].reshape(1, 1, 3 * d), (bsz, 1, 3 * d))
        npre = norm_pre[l].reshape(1, d)
        npost = norm_post[l].reshape(1, d)
        zc = _inproj(hc, mc[..., :d], mc[..., d:2 * d], npre, w_in_p[l], colmajor=False)
        zx = _inproj(h, mx[..., :d], mx[..., d:2 * d], npre, w_in_p[l], colmajor=colmajor)

        wg = jnp.concatenate([_block_diag_halves(rg_w_r[l, 0]), _block_diag_halves(rg_w_i[l, 0]),
                              _block_diag_halves(rg_w_r[l, 1]), _block_diag_halves(rg_w_i[l, 1])],
                             axis=-1).astype(BF16)
        ya_c, ya_x = _rglru(zc, zx, conv_a_w[l], conv_a_b[l].reshape(1, dh), wg,
                            rg_b_r[l], rg_b_i[l], rg_lam[l])

        prm = jnp.zeros((SUBLANES, LANES), F32)
        prm = prm.at[0, :2 * H_B].set(gdn_a_log[l].reshape(-1)).at[1, :2 * H_B].set(gdn_dt_bias[l].reshape(-1))
        yb_c, yb_x = _gdn(zc, zx, conv_b_w[l], prm, gdn_norm[l].reshape(1, LANES))

        yc_c, yc_x = _gla(zc, zx, lb2, hg_norm[l].reshape(1, LANES), msum, mscore, l)

        h = _outproj(ya_x, yb_x, yc_x, w_out_b[l], h, mx[..., 2 * d:], npost, colmajor=colmajor)
        if l < depth - 1:
            hc = _outproj(ya_c, yb_c, yc_c, w_out_b[l], hc, mc[..., 2 * d:], npost, colmajor=False)
    return h
```

```python
import functools

import numpy as np
import jax
import jax.numpy as jnp
from jax import lax
from jax.experimental import pallas as pl
from jax.experimental.pallas import tpu as pltpu

F32 = jnp.float32
BF16 = jnp.bfloat16
HI = lax.Precision.HIGHEST

GRID_W = 64
H_A, HD_A = 8, 64
H_B, DK_B = 4, 128
H_C, DK_C = 4, 128
D_HEADS = 512
CONV_W = 4
CONV_PAD = 2
RG_C = 8.0
EPS = 1e-6

LANES = 128
SUBLANES = 8
ROW_TILE = 256
PROJ_TILE = 512
CHUNK_B = 64
TILE_C = 128
VMEM_LIMIT = 56 * 1024 * 1024

COL_XA, COL_GA, COL_Q, COL_K, COL_V, COL_GB = 0, 4, 8, 12, 16, 20
COL_QC, COL_FF, COL_FB, COL_IC, COL_GC, COL_AB = 24, 28, 32, 36, 40, 44
N_PROJ = 45 * LANES


def _dot(a, b):
    return jnp.dot(a.astype(BF16), b.astype(BF16), preferred_element_type=F32)


def _dot_nt(a, b):
    return lax.dot_general(a.astype(BF16), b.astype(BF16), (((1,), (1,)), ((), ())),
                           preferred_element_type=F32)


def _dot_tn(a, b):
    return lax.dot_general(a.astype(BF16), b.astype(BF16), (((0,), (0,)), ((), ())),
                           preferred_element_type=F32)


def _dot_hi(a, b):
    return jnp.dot(a, b, precision=HI, preferred_element_type=F32)


def _sigmoid(x):
    return 1.0 / (1.0 + jnp.exp(-x))


def _silu(x):
    return x * _sigmoid(x)


def _softplus(x):
    return jnp.maximum(x, 0.0) + jnp.log1p(jnp.exp(-jnp.abs(x)))


def _rms(x):
    return x * lax.rsqrt(jnp.mean(x * x, axis=-1, keepdims=True) + EPS)


def _params(**kw):
    return pltpu.CompilerParams(vmem_limit_bytes=VMEM_LIMIT, **kw)


def _ada_kernel(s_ref, w_ref, b_ref, o_ref):
    o_ref[...] = _dot_hi(_silu(s_ref[...]), w_ref[...]) + b_ref[...]


def _ada(cs, w_ada, b_ada):
    depth, d, n3 = w_ada.shape
    tn = 1024
    return pl.pallas_call(
        _ada_kernel,
        grid=(depth, n3 // tn),
        in_specs=[pl.BlockSpec((16, d), lambda l, j: (0, 0)),
                  pl.BlockSpec((None, d, tn), lambda l, j: (l, 0, j)),
                  pl.BlockSpec((None, 1, tn), lambda l, j: (l, 0, j))],
        out_specs=pl.BlockSpec((None, 16, tn), lambda l, j: (l, 0, j)),
        out_shape=jax.ShapeDtypeStruct((depth, 16, n3), F32),
        compiler_params=_params(),
        name="ada",
    )(cs, w_ada, b_ada.reshape(depth, 1, n3))


def _inproj_kernel(h_ref, sh_ref, sc_ref, npre_ref, w_ref, z_ref, *, colmajor, cn):
    if colmajor:
        x = jnp.concatenate([h_ref[:, j, :] for j in range(h_ref.shape[1])], axis=0)
    else:
        x = h_ref[...]
    u = (_rms(x) * npre_ref[...] * (1.0 + sc_ref[...]) + sh_ref[...]).astype(BF16)
    for j in range(z_ref.shape[-1] // cn):
        z_ref[:, j * cn:(j + 1) * cn] = jnp.dot(u, w_ref[:, j * cn:(j + 1) * cn],
                                               preferred_element_type=F32)


def _inproj(h, shift, scale, npre, w, *, colmajor):
    bsz, t_len, d = h.shape
    n = w.shape[-1]
    tc = min(PROJ_TILE, t_len)
    if colmajor:
        rows = t_len // GRID_W
        h_in = h.reshape(bsz, rows, GRID_W, d)
        h_spec = pl.BlockSpec((None, rows, tc // rows, d), lambda b, i: (b, 0, i, 0))
    else:
        h_in = h
        h_spec = pl.BlockSpec((None, tc, d), lambda b, i: (b, i, 0))
    vec = pl.BlockSpec((None, 1, d), lambda b, i: (b, 0, 0))
    return pl.pallas_call(
        functools.partial(_inproj_kernel, colmajor=colmajor, cn=1152),
        grid=(bsz, t_len // tc),
        in_specs=[h_spec, vec, vec,
                  pl.BlockSpec((1, d), lambda b, i: (0, 0)),
                  pl.BlockSpec((d, n), lambda b, i: (0, 0), pipeline_mode=pl.Buffered(1))],
        out_specs=pl.BlockSpec((None, tc, n), lambda b, i: (b, i, 0)),
        out_shape=jax.ShapeDtypeStruct((bsz, t_len, n), F32),
        compiler_params=_params(),
        name="inproj",
    )(h_in, shift, scale, npre, w)


def _outproj_kernel(ya_ref, yb_ref, yc_ref, w_ref, h_ref, gt_ref, npost_ref, o_ref, *, colmajor):
    dm = ya_ref.shape[-1]
    d = npost_ref.shape[-1]
    o = (jnp.dot(ya_ref[...].astype(BF16), w_ref[0:dm, :], preferred_element_type=F32)
         + jnp.dot(yb_ref[...].astype(BF16), w_ref[dm:2 * dm, :], preferred_element_type=F32)
         + jnp.dot(yc_ref[...].astype(BF16), w_ref[2 * dm:3 * dm, :], preferred_element_type=F32))
    upd = gt_ref[...] * (_rms(o) * npost_ref[...])
    if colmajor:
        rows = h_ref.shape[0]
        for j in range(h_ref.shape[1]):
            o_ref[:, j, :] = h_ref[:, j, :] + upd[j * rows:(j + 1) * rows, :]
    else:
        o_ref[...] = h_ref[...] + upd


def _outproj(ya, yb, yc, w, h, gate, npost, *, colmajor):
    bsz, t_len, d = h.shape
    dm = ya.shape[-1]
    tc = min(PROJ_TILE, t_len)
    if colmajor:
        rows = t_len // GRID_W
        h_in = h.reshape(bsz, rows, GRID_W, d)
        h_spec = pl.BlockSpec((None, rows, tc // rows, d), lambda b, i: (b, 0, i, 0))
    else:
        h_in = h
        h_spec = pl.BlockSpec((None, tc, d), lambda b, i: (b, i, 0))
    y_spec = pl.BlockSpec((None, tc, dm), lambda b, i: (b, i, 0))
    out = pl.pallas_call(
        functools.partial(_outproj_kernel, colmajor=colmajor),
        grid=(bsz, t_len // tc),
        in_specs=[y_spec, y_spec, y_spec,
                  pl.BlockSpec((3 * dm, d), lambda b, i: (0, 0)),
                  h_spec,
                  pl.BlockSpec((None, 1, d), lambda b, i: (b, 0, 0)),
                  pl.BlockSpec((1, d), lambda b, i: (0, 0))],
        out_specs=h_spec,
        out_shape=jax.ShapeDtypeStruct(h_in.shape, F32),
        compiler_params=_params(),
        name="outproj",
    )(ya, yb, yc, w, h_in, gate, npost)
    return out.reshape(bsz, t_len, d)


def _conv_tile(pad_ref, cw_ref, i):
    base = i * ROW_TILE + SUBLANES - CONV_PAD
    acc = cw_ref[0:1, :] * pad_ref[base:base + ROW_TILE, :]
    for j in range(1, CONV_W):
        acc = acc + cw_ref[j:j + 1, :] * pad_ref[base + j:base + j + ROW_TILE, :]
    return acc


def _fill_pad(pad_ref, x_ref, t_len):
    zeros = jnp.zeros((SUBLANES, pad_ref.shape[-1]), F32)
    pad_ref[0:SUBLANES, :] = zeros
    pad_ref[SUBLANES:SUBLANES + t_len, :] = x_ref[...]
    pad_ref[SUBLANES + t_len:2 * SUBLANES + t_len, :] = zeros


def _rglru_kernel(xc_ref, gc_ref, xx_ref, gx_ref, cw_ref, cb_ref, wg_ref, br_ref, bi_ref, lam_ref,
                  oc_ref, ox_ref, pad_ref, h_s):
    dh = xc_ref.shape[-1]
    half = dh // 2
    nv = ROW_TILE // SUBLANES
    sp = _softplus(-lam_ref[...])
    sub = lax.broadcasted_iota(jnp.int32, (SUBLANES, dh), 0)

    nlb = dh // LANES

    def gates(i, dirn):
        base = i * ROW_TILE + SUBLANES - CONV_PAD
        vr = [jnp.concatenate([pad_ref[lb, pl.ds(base + j, SUBLANES, stride=nv), :] for lb in range(nlb)], axis=1)
              for j in range(nv + CONV_W - 1)]
        xc = cb_ref[...] + cw_ref[0:1, :] * jnp.concatenate(vr[0:nv], axis=0)
        for tap in range(1, CONV_W):
            xc = xc + cw_ref[tap:tap + 1, :] * jnp.concatenate(vr[tap:tap + nv], axis=0)
        pre = [_dot(xc[:, hf * half:(hf + 1) * half], wg_ref[hf, :, dirn * dh:(dirn + 1) * dh])
               for hf in range(2)]
        r_pre = jnp.concatenate([pre[0][:, :half], pre[1][:, :half]], axis=1)
        i_pre = jnp.concatenate([pre[0][:, half:], pre[1][:, half:]], axis=1)
        r = _sigmoid(r_pre + br_ref[dirn:dirn + 1, :])
        ig = _sigmoid(i_pre + bi_ref[dirn:dirn + 1, :])
        log_a = -RG_C * r * sp[dirn:dirn + 1, :]
        a = jnp.exp(log_a)
        y = -jnp.tanh(log_a) * (a * a + 1.0)
        u = jnp.where(y > 0.0, y * lax.rsqrt(y), 0.0) * (ig * xc)
        return a, u

    def tile_scan(a, u, carry, rev):
        order = list(range(nv))[::-1] if rev else list(range(nv))
        a_cum, h_loc = [None] * nv, [None] * nv
        prev = None
        for j in order:
            aj, uj = a[j * SUBLANES:(j + 1) * SUBLANES, :], u[j * SUBLANES:(j + 1) * SUBLANES, :]
            if prev is None:
                a_cum[j], h_loc[j] = aj, uj
            else:
                a_cum[j], h_loc[j] = aj * a_cum[prev], aj * h_loc[prev] + uj
            prev = j
        p, r = a_cum[prev], h_loc[prev]
        for sft in (1, 2, 4):
            keep = (sub < SUBLANES - sft) if rev else (sub >= sft)
            amt = SUBLANES - sft if rev else sft
            r = r + p * jnp.where(keep, pltpu.roll(r, amt, 0), 0.0)
            p = p * jnp.where(keep, pltpu.roll(p, amt, 0), 1.0)
        state = p * carry + r
        keep = (sub < SUBLANES - 1) if rev else (sub >= 1)
        c_in = jnp.where(keep, pltpu.roll(state, SUBLANES - 1 if rev else 1, 0), carry)
        last = 0 if rev else SUBLANES - 1
        return [h_loc[j] + a_cum[j] * c_in for j in range(nv)], state[last:last + 1, :]

    def seg(x_ref, g_ref, o_ref, t_len, h0):
        n_tiles = t_len // ROW_TILE
        zeros = jnp.zeros((SUBLANES, LANES), F32)
        for lb in range(nlb):
            pad_ref[lb, 0:SUBLANES, :] = zeros
            pad_ref[lb, SUBLANES:SUBLANES + t_len, :] = x_ref[:, lb * LANES:(lb + 1) * LANES]
            pad_ref[lb, SUBLANES + t_len:2 * SUBLANES + t_len, :] = zeros
        carries = []
        for dirn in range(2):
            def tile_body(it, carry, dirn=dirn):
                i = it if dirn == 0 else n_tiles - 1 - it
                a, u = gates(i, dirn)
                hs, carry = tile_scan(a, u, carry, dirn == 1)
                for j in range(nv):
                    for lb in range(nlb):
                        h_s[dirn, lb, pl.ds(i * ROW_TILE + j, SUBLANES, stride=nv), :] = (
                            hs[j][:, lb * LANES:(lb + 1) * LANES])
                return carry

            carries.append(lax.fori_loop(0, n_tiles, tile_body, h0[dirn]))
        for i in range(n_tiles):
            rows = slice(i * ROW_TILE, (i + 1) * ROW_TILE)
            hsum = jnp.concatenate([h_s[0, lb, rows, :] + h_s[1, lb, rows, :] for lb in range(nlb)], axis=1)
            o_ref[rows, :] = hsum * _silu(g_ref[rows, :])
        return tuple(carries)

    zero = jnp.zeros((1, dh), F32)
    states = seg(xc_ref, gc_ref, oc_ref, xc_ref.shape[0], (zero, zero))
    seg(xx_ref, gx_ref, ox_ref, xx_ref.shape[0], states)


def _rglru(zc, zx, cw, cb, wg, br, bi, lam):
    bsz, tc_len, _ = zc.shape
    tx_len = zx.shape[1]
    dh = D_HEADS
    nb = dh // LANES

    def zspec(t_len, col):
        return pl.BlockSpec((None, t_len, dh), lambda b: (b, 0, col // nb))

    def full(a):
        return pl.BlockSpec(a.shape, lambda b: (0,) * a.ndim)

    return pl.pallas_call(
        _rglru_kernel,
        grid=(bsz,),
        in_specs=[zspec(tc_len, COL_XA), zspec(tc_len, COL_GA), zspec(tx_len, COL_XA), zspec(tx_len, COL_GA),
                  full(cw), full(cb), full(wg), full(br), full(bi), full(lam)],
        out_specs=[pl.BlockSpec((None, tc_len, dh), lambda b: (b, 0, 0)),
                   pl.BlockSpec((None, tx_len, dh), lambda b: (b, 0, 0))],
        out_shape=[jax.ShapeDtypeStruct((bsz, tc_len, dh), F32),
                   jax.ShapeDtypeStruct((bsz, tx_len, dh), F32)],
        scratch_shapes=[pltpu.VMEM((nb, tx_len + 2 * SUBLANES, LANES), F32),
                        pltpu.VMEM((2, nb, tx_len, LANES), F32)],
        compiler_params=_params(),
        name="rglru",
    )(zc, zc, zx, zx, cw, cb, wg, br, bi, lam)


GDN_UNROLL = 8
GDN_SPLICES = 4


def _gdn_masks():
    c = CHUNK_B
    i = np.arange(c)[:, None]
    k = np.arange(c)[None, :]
    stacked = np.stack([np.concatenate([k <= i, k > i], 0), np.concatenate([k >= i, k < i], 0)])
    return np.tile(stacked, (1, 1, 3)).astype(np.float32)


def _split3_rows(x):
    hi = x.astype(BF16)
    r1 = x - hi.astype(F32)
    mid = r1.astype(BF16)
    lo = (r1 - mid.astype(F32)).astype(BF16)
    return jnp.concatenate([hi, mid, lo], axis=0)


INV_BLOCK = 16


def _gdn_kernel(qc_ref, kc_ref, vc_ref, abc_ref, gtc_ref, qx_ref, kx_ref, vx_ref, abx_ref, gtx_ref,
                cwq_ref, cwk_ref, cwv_ref, prm_ref, nw_ref, mst_ref, oc_ref, ox_ref,
                pad_ref, q_s, k_s, v_s, gf_s, bf_s, gb_s, bb_s, o_s, qp_s, p_s, n_s, et_s):
    head = pl.program_id(1)
    c = CHUNK_B
    ii = lax.broadcasted_iota(jnp.int32, (c, c), 0)
    jj = lax.broadcasted_iota(jnp.int32, (c, c), 1)
    eye = (ii == jj).astype(F32)
    a_mask = (ii > jj, ii < jj)
    s_mask = (ii >= jj, ii <= jj)
    d_mask = (ii // INV_BLOCK) == (jj // INV_BLOCK)
    f_mask = []
    size = INV_BLOCK
    while size < c:
        same = (ii // (2 * size)) == (jj // (2 * size))
        hi_i, hi_j = (ii // size) % 2, (jj // size) % 2
        f_mask.append((same & (hi_i == 1) & (hi_j == 0), same & (hi_i == 0) & (hi_j == 1)))
        size *= 2
    sel_r = lax.broadcasted_iota(jnp.int32, (3 * LANES, 4 * LANES), 0) % LANES
    sel_b = lax.broadcasted_iota(jnp.int32, (3 * LANES, 4 * LANES), 1) // LANES
    sel4 = (sel_r == head + H_B * (2 * (sel_b % 2) + sel_b // 2)).astype(BF16)
    lane = lax.broadcasted_iota(jnp.int32, (ROW_TILE, LANES), 1)

    def prep(q_ref, k_ref, v_ref, ab_ref, t_len):
        n_tiles = t_len // ROW_TILE
        for src, dst, cw_ref, kind in ((q_ref, q_s, cwq_ref, "q"), (k_ref, k_s, cwk_ref, "k"),
                                       (v_ref, v_s, cwv_ref, "v")):
            _fill_pad(pad_ref, src, t_len)
            for i in range(n_tiles):
                xc = _silu(_conv_tile(pad_ref, cw_ref, i))
                if kind != "v":
                    xc = xc * lax.rsqrt(jnp.sum(xc * xc, axis=-1, keepdims=True) + EPS)
                if kind == "q":
                    xc = xc * (DK_B ** -0.5)
                dst[i * ROW_TILE:(i + 1) * ROW_TILE, :] = xc
        for i in range(n_tiles):
            rows = slice(i * ROW_TILE, (i + 1) * ROW_TILE)
            ab = ab_ref[rows, :]
            gval = -jnp.exp(prm_ref[0:1, :]) * _softplus(ab + prm_ref[1:2, :])
            gbv = jnp.where(lane < 2 * H_B, gval, _sigmoid(ab))
            hi = gbv.astype(BF16)
            r1 = gbv - hi.astype(F32)
            mid = r1.astype(BF16)
            lo = (r1 - mid.astype(F32)).astype(BF16)
            bc = jnp.dot(jnp.concatenate([hi, mid, lo], axis=1), sel4, preferred_element_type=F32)
            for j, dst in enumerate((gf_s, bf_s, gb_s, bb_s)):
                dst[rows, :] = bc[:, j * LANES:(j + 1) * LANES]

    def chunk_terms(items, hooks=()):
        hooks = list(hooks)
        per_splice = -(-len(hooks) // GDN_SPLICES)

        def splice():
            for _ in range(min(per_splice, len(hooks))):
                hooks.pop(0)()

        m = range(len(items))
        dirs = [d for d, _ in items]
        rows = [pl.ds(pl.multiple_of(ci * c, c), c) for _, ci in items]
        prow = [pl.ds(pl.multiple_of(ci * LANES, LANES), LANES) for _, ci in items]
        erow = [pl.ds(pl.multiple_of(ci * SUBLANES, SUBLANES), SUBLANES) for _, ci in items]
        q = [q_s[r, :] for r in rows]
        k = [k_s[r, :] for r in rows]
        v = [v_s[r, :] for r in rows]
        g = [(gf_s, gb_s)[d][r, :] for d, r in zip(dirs, rows)]
        beta = [(bf_s, bb_s)[d][r, :] for d, r in zip(dirs, rows)]
        cum = [jnp.dot(mst_ref[dirs[i]], _split3_rows(g[i]), preferred_element_type=F32) for i in m]
        kb = [k[i] * beta[i] for i in m]
        kq = [_dot_nt(jnp.concatenate([kb[i], q[i]], axis=0), k[i]) for i in m]
        splice()
        e_in = [jnp.exp(cum[i][0:c]) for i in m]
        e_rest = [jnp.exp(cum[i][c:2 * c]) for i in m]
        cs = [cum[i][0:c, 0:c] for i in m]
        dec = [jnp.exp(jnp.where(s_mask[dirs[i]], cs[i] - cs[i].T, 0.0)) for i in m]
        for i in m:
            et_s[dirs[i], erow[i], :] = jnp.exp((cum[i][0:SUBLANES] + cum[i][c:c + SUBLANES]))
        a = [jnp.where(a_mask[dirs[i]], kq[i][:c] * dec[i], 0.0) for i in m]
        qk = [jnp.where(s_mask[dirs[i]], kq[i][c:] * dec[i], 0.0) for i in m]
        dg = [jnp.where(d_mask, a[i], 0.0) for i in m]
        x1 = [_dot(dg[i], dg[i]) for i in m]
        x2 = [_dot(x1[i], x1[i]) for i in m]
        splice()
        st = [_dot(jnp.concatenate([x1[i], x2[i]], axis=0), x2[i]) for i in m]
        q2 = [x1[i] + x2[i] + st[i][:c] for i in m]
        q3 = [q2[i] + st[i][c:] + _dot(q2[i], st[i][c:]) for i in m]
        imd = [eye - dg[i] for i in m]
        t_inv = [imd[i] + _dot(imd[i], q3[i]) for i in m]
        splice()
        for fm in f_mask:
            tf = [_dot(t_inv[i], jnp.where(fm[dirs[i]], a[i], 0.0)) for i in m]
            t_inv = [t_inv[i] - _dot(tf[i], t_inv[i]) for i in m]
        splice()
        uw = [_dot(t_inv[i], jnp.concatenate([v[i] * beta[i], kb[i] * e_in[i]], axis=1)) for i in m]
        res = [_dot(jnp.concatenate([qk[i], (k[i] * e_rest[i]).T], axis=0), uw[i]) for i in m]
        for i in m:
            n_s[dirs[i], prow[i], :] = res[i][c:, :LANES]
            p_s[dirs[i], prow[i], :] = res[i][c:, LANES:].astype(BF16)
            o_s[dirs[i], rows[i], :] = res[i][:c, :LANES]
            qp_s[dirs[i], rows[i], :] = (q[i] * e_in[i] - res[i][:c, LANES:]).astype(BF16)

    def seg(q_ref, k_ref, v_ref, ab_ref, gt_ref, o_ref, states):
        t_len = q_ref.shape[0]
        n_chunks = t_len // c
        prep(q_ref, k_ref, v_ref, ab_ref, t_len)

        unroll = min(GDN_UNROLL, n_chunks)
        n_pass = n_chunks // unroll

        def terms(p, hooks=()):
            chunk_terms([(dirn, p * unroll + j if dirn == 0 else n_chunks - 1 - (p * unroll + j))
                         for j in range(unroll) for dirn in range(2)], hooks)

        def state_step(n, st):
            new = []
            for dirn, ci in ((0, n), (1, n_chunks - 1 - n)):
                rows = pl.ds(pl.multiple_of(ci * c, c), c)
                prow = pl.ds(pl.multiple_of(ci * LANES, LANES), LANES)
                sb = st[dirn].astype(BF16)
                o_s[dirn, rows, :] = o_s[dirn, rows, :] + jnp.dot(qp_s[dirn, rows, :], sb,
                                                                 preferred_element_type=F32)
                e_tot = et_s[dirn, pl.ds(pl.multiple_of(ci * SUBLANES, SUBLANES), 1), :]
                new.append(st[dirn] * e_tot + n_s[dirn, prow, :]
                           - jnp.dot(p_s[dirn, prow, :], sb, preferred_element_type=F32))
            return tuple(new)

        terms(0)

        def pass_body(p, st):
            holder = [st]

            def hook(j):
                def run():
                    holder[0] = state_step((p - 1) * unroll + j, holder[0])
                return run

            terms(p, [hook(j) for j in range(unroll)])
            return holder[0]

        states = lax.fori_loop(1, n_pass, pass_body, states)
        for j in range(unroll):
            states = state_step((n_pass - 1) * unroll + j, states)
        for i in range(t_len // ROW_TILE):
            rows = slice(i * ROW_TILE, (i + 1) * ROW_TILE)
            o = o_s[0, rows, :] + o_s[1, rows, :]
            o_ref[rows, :] = _rms(o) * nw_ref[...] * _silu(gt_ref[rows, :])
        return states

    zero = jnp.zeros((LANES, LANES), F32)
    states = seg(qc_ref, kc_ref, vc_ref, abc_ref, gtc_ref, oc_ref, (zero, zero))
    seg(qx_ref, kx_ref, vx_ref, abx_ref, gtx_ref, ox_ref, states)


def _gdn(zc, zx, cwb, prm, nw):
    bsz, tc_len, _ = zc.shape
    tx_len = zx.shape[1]

    def zspec(t_len, col, per_head=True):
        if per_head:
            return pl.BlockSpec((None, t_len, LANES), lambda b, h: (b, 0, col + h))
        return pl.BlockSpec((None, t_len, LANES), lambda b, h: (b, 0, col))

    def seg_specs(t_len):
        return [zspec(t_len, COL_Q), zspec(t_len, COL_K), zspec(t_len, COL_V),
                zspec(t_len, COL_AB, False), zspec(t_len, COL_GB)]

    def cw_spec(off):
        return pl.BlockSpec((CONV_W, LANES), lambda b, h: (0, off + h))

    mst = jnp.asarray(_gdn_masks(), BF16)
    n_chunks = tx_len // CHUNK_B
    return pl.pallas_call(
        _gdn_kernel,
        grid=(bsz, H_B),
        in_specs=seg_specs(tc_len) + seg_specs(tx_len) + [
            cw_spec(0), cw_spec(H_B), cw_spec(2 * H_B),
            pl.BlockSpec(prm.shape, lambda b, h: (0, 0)),
            pl.BlockSpec(nw.shape, lambda b, h: (0, 0)),
            pl.BlockSpec(mst.shape, lambda b, h: (0, 0, 0))],
        out_specs=[pl.BlockSpec((None, tc_len, LANES), lambda b, h: (b, 0, h)),
                   pl.BlockSpec((None, tx_len, LANES), lambda b, h: (b, 0, h))],
        out_shape=[jax.ShapeDtypeStruct((bsz, tc_len, D_HEADS), F32),
                   jax.ShapeDtypeStruct((bsz, tx_len, D_HEADS), F32)],
        scratch_shapes=[pltpu.VMEM((tx_len + 2 * SUBLANES, LANES), F32)]
                       + [pltpu.VMEM((tx_len, LANES), F32) for _ in range(7)]
                       + [pltpu.VMEM((2, tx_len, LANES), F32),
                          pltpu.VMEM((2, tx_len, LANES), BF16),
                          pltpu.VMEM((2, n_chunks * LANES, LANES), BF16),
                          pltpu.VMEM((2, n_chunks * LANES, LANES), F32),
                          pltpu.VMEM((2, n_chunks * SUBLANES, LANES), F32)],
        compiler_params=_params(),
        name="gdn",
    )(zc, zc, zc, zc, zc, zx, zx, zx, zx, zx, cwb, cwb, cwb, prm, nw, mst)


_LEVELS = (1, 2, 4, 8, 16, 32, 64)


def _gla_masks():
    n = TILE_C
    i = np.arange(n)[:, None]
    k = np.arange(n)[None, :]
    sums, scores = [], []
    for lv in _LEVELS:
        same = (i // (2 * lv)) == (k // (2 * lv))
        hi_i, hi_k = (i // lv) % 2, (k // lv) % 2
        if lv < SUBLANES:
            sums.append(same & np.where(hi_i == 1, (hi_k == 1) & (k <= i), (hi_k == 0) & (k > i)))
        scores.append(same & (hi_i == 1) & (hi_k == 0))
    sums += [k <= i, k > i]
    scores.append(i == k)
    sums_f = np.stack(sums).astype(np.float32)
    scores_f = np.stack(scores).astype(np.float32)
    sums_all = np.stack([sums_f, sums_f[:, ::-1, ::-1]]).reshape(2, len(sums) * n, n)
    scores_all = np.stack([scores_f, scores_f[:, ::-1, ::-1]])
    return np.tile(sums_all, (1, 1, 2)), scores_all


GLA_UNROLL = 4


def _gla_kernel(qc_ref, ffc_ref, fbc_ref, ic_ref, gtc_ref, qx_ref, ffx_ref, fbx_ref, ix_ref, gtx_ref,
                lb_ref, nw_ref, msum_ref, mscore_ref, oc_ref, ox_ref, oi_s, qd_s, u_s, et_s, *, layer):
    n = TILE_C
    n_lv = len(_LEVELS)
    depth = lb_ref.shape[0] // 2
    lbs = []
    for dirn in range(2):
        rows = [lb_ref[2 * j + dirn:2 * j + dirn + 1, :] for j in range(depth)]
        mx = functools.reduce(jnp.maximum, rows)
        ex = [jnp.exp(r - mx) for r in rows]
        den = functools.reduce(lambda p, q: p + q, ex)
        acc = jnp.zeros_like(mx)
        for j in range(1, layer + 1):
            acc = acc + ex[j] / den
        lbs.append(acc)

    def tile_terms(items, q_ref, f_refs, i_ref):
        m = range(len(items))
        dirs = [d for d, _ in items]
        rows = [pl.ds(pl.multiple_of(ti * n, n), n) for _, ti in items]
        q = [_silu(q_ref[r, :]) for r in rows]
        v = [i_ref[r, :] for r in rows]
        fg = [lbs[dirs[i]] + (1.0 - lbs[dirs[i]]) * _sigmoid(f_refs[dirs[i]][rows[i], :]) for i in m]
        k = [1.0 - fg[i] for i in m]
        lf = [jnp.log(fg[i]) for i in m]
        hi = [lf[i].astype(BF16) for i in m]
        lf2 = [jnp.concatenate([hi[i], (lf[i] - hi[i].astype(F32)).astype(BF16)], axis=0) for i in m]

        sums = [jnp.dot(msum_ref[dirs[i]], lf2[i], preferred_element_type=F32) for i in m]

        def csum(j):
            return [sums[i][j * n:(j + 1) * n, :] for i in m]

        n_small = sum(1 for lv in _LEVELS if lv < SUBLANES)
        c_in, c_rest = csum(n_small), csum(n_small + 1)

        def boundary_exponent(i, lv):
            off = lv - 1 if dirs[i] == 0 else lv
            ref = jnp.concatenate(
                [jnp.broadcast_to(c_in[i][blk * 2 * lv + off:blk * 2 * lv + off + 1, :], (2 * lv, LANES))
                 for blk in range(n // (2 * lv))], axis=0)
            return -jnp.abs(c_in[i] - ref)

        qb = [q[i].astype(BF16) for i in m]
        kb = [k[i].astype(BF16) for i in m]
        scores = [mscore_ref[dirs[i], n_lv] * _dot_nt(qb[i], kb[i]).astype(BF16) for i in m]
        for j, lv in enumerate(_LEVELS):
            xs = csum(j) if lv < SUBLANES else [boundary_exponent(i, lv) for i in m]
            e = [jnp.exp(x).astype(BF16) for x in xs]
            p = [_dot_nt(qb[i] * e[i], kb[i] * e[i]) for i in m]
            scores = [scores[i] + mscore_ref[dirs[i], j] * p[i].astype(BF16) for i in m]
        o_intra = [_dot(scores[i], v[i]) for i in m]
        u = [_dot_tn(v[i], k[i] * jnp.exp(c_rest[i])) for i in m]
        for i, (_, ti) in enumerate(items):
            oi_s[dirs[i], rows[i], :] = o_intra[i]
            qd_s[dirs[i], rows[i], :] = (q[i] * jnp.exp(c_in[i])).astype(BF16)
            u_s[dirs[i], rows[i], :] = u[i]
            et_s[dirs[i], pl.ds(pl.multiple_of(ti * SUBLANES, SUBLANES), SUBLANES), :] = jnp.exp(
                (c_in[i] + c_rest[i])[0:SUBLANES, :])

    def seg(q_ref, ff_ref, fb_ref, i_ref, gt_ref, o_ref, states):
        t_len = q_ref.shape[0]
        n_tiles = t_len // n

        unroll = min(GLA_UNROLL, n_tiles)
        n_pass = n_tiles // unroll

        def terms(p):
            tile_terms([(dirn, p * unroll + j if dirn == 0 else n_tiles - 1 - (p * unroll + j))
                        for j in range(unroll) for dirn in range(2)], q_ref, (ff_ref, fb_ref), i_ref)

        def state_steps(p, st):
            for j in range(unroll):
                ti = p * unroll + j
                new = []
                for dirn, tj in ((0, ti), (1, n_tiles - 1 - ti)):
                    rows = pl.ds(pl.multiple_of(tj * n, n), n)
                    oi_s[dirn, rows, :] = oi_s[dirn, rows, :] + lax.dot_general(
                        qd_s[dirn, rows, :], st[dirn].astype(BF16), (((1,), (1,)), ((), ())),
                        preferred_element_type=F32)
                    e_tot = et_s[dirn, pl.ds(pl.multiple_of(tj * SUBLANES, SUBLANES), 1), :]
                    new.append(st[dirn] * e_tot + u_s[dirn, rows, :])
                st = tuple(new)
            return st

        terms(0)

        def pass_body(p, st):
            st = state_steps(p - 1, st)
            terms(p)
            return st

        states = lax.fori_loop(1, n_pass, pass_body, states)
        states = state_steps(n_pass - 1, states)
        for i in range(t_len // ROW_TILE):
            rows = slice(i * ROW_TILE, (i + 1) * ROW_TILE)
            o = oi_s[0, rows, :] + oi_s[1, rows, :]
            o_ref[rows, :] = _rms(o) * nw_ref[...] * _silu(gt_ref[rows, :])
        return states

    zero = jnp.zeros((LANES, LANES), F32)
    states = seg(qc_ref, ffc_ref, fbc_ref, ic_ref, gtc_ref, oc_ref, (zero, zero))
    seg(qx_ref, ffx_ref, fbx_ref, ix_ref, gtx_ref, ox_ref, states)


def _gla(zc, zx, lb, nw, msum, mscore, layer):
    bsz, tc_len, _ = zc.shape
    tx_len = zx.shape[1]

    def zspec(t_len, col):
        return pl.BlockSpec((None, t_len, LANES), lambda b, h: (b, 0, col + h))

    def seg_specs(t_len):
        return [zspec(t_len, COL_QC), zspec(t_len, COL_FF), zspec(t_len, COL_FB),
                zspec(t_len, COL_IC), zspec(t_len, COL_GC)]

    return pl.pallas_call(
        functools.partial(_gla_kernel, layer=layer),
        grid=(bsz, H_C),
        in_specs=seg_specs(tc_len) + seg_specs(tx_len) + [
            pl.BlockSpec((lb.shape[0], LANES), lambda b, h: (0, h)),
            pl.BlockSpec(nw.shape, lambda b, h: (0, 0)),
            pl.BlockSpec(msum.shape, lambda b, h: (0, 0, 0)),
            pl.BlockSpec(mscore.shape, lambda b, h: (0, 0, 0, 0))],
        out_specs=[pl.BlockSpec((None, tc_len, LANES), lambda b, h: (b, 0, h)),
                   pl.BlockSpec((None, tx_len, LANES), lambda b, h: (b, 0, h))],
        out_shape=[jax.ShapeDtypeStruct((bsz, tc_len, D_HEADS), F32),
                   jax.ShapeDtypeStruct((bsz, tx_len, D_HEADS), F32)],
        scratch_shapes=[pltpu.VMEM((2, tx_len, LANES), F32),
                        pltpu.VMEM((2, tx_len, LANES), BF16),
                        pltpu.VMEM((2, tx_len, LANES), F32),
                        pltpu.VMEM((2, (tx_len // TILE_C) * SUBLANES, LANES), F32)],
        compiler_params=_params(),
        name="hgrn2",
    )(zc, zc, zc, zc, zc, zx, zx, zx, zx, zx, lb, nw, msum, mscore)


def _block_diag_halves(w):
    w4 = w.reshape(2, H_A // 2, HD_A, HD_A)
    eye = jnp.eye(H_A // 2, dtype=w.dtype)
    return jnp.einsum("ghij,hk->ghikj", w4, eye).reshape(2, (H_A // 2) * HD_A, (H_A // 2) * HD_A)


def kernel(x, c, ctx, c_ctx, w_ada, b_ada, norm_pre, norm_post, w_in, conv_a_w, conv_a_b, rg_w_r, rg_b_r,
           rg_w_i, rg_b_i, rg_lam, conv_b_w, gdn_a_log, gdn_dt_bias, gdn_norm, hg_lb, hg_norm, w_out):
    bsz, t_len, d = x.shape
    depth = w_ada.shape[0]
    dh = D_HEADS

    cs = jnp.concatenate([c, c_ctx[None, :], jnp.zeros((16 - bsz - 1, d), F32)], axis=0)
    mod = _ada(cs, w_ada, b_ada)

    qkv_end = 2 * dh + 3 * dh
    w_in_b = w_in.astype(BF16)
    w_in_p = jnp.concatenate(
        [w_in_b[:, :, :qkv_end], w_in_b[:, :, qkv_end + 4 * H_B:], w_in_b[:, :, qkv_end:qkv_end + 4 * H_B],
         jnp.zeros((depth, d, LANES - 4 * H_B), BF16)], axis=-1)
    w_out_b = w_out.astype(BF16)
    msum_np, mscore_np = _gla_masks()
    msum = jnp.asarray(msum_np, BF16)
    mscore = jnp.asarray(mscore_np, BF16)
    lb2 = hg_lb.reshape(depth * 2, dh)

    h, hc = x, ctx
    for l in range(depth):
        colmajor = l % 2 == 1
        mx = mod[l, :bsz].reshape(bsz, 1, 3 * d)
        mc = jnp.broadcast_to(mod[l, bsz].reshape(1, 1, 3 * d), (bsz, 1, 3 * d))
        npre = norm_pre[l].reshape(1, d)
        npost = norm_post[l].reshape(1, d)
        zc = _inproj(hc, mc[..., :d], mc[..., d:2 * d], npre, w_in_p[l], colmajor=False)
        zx = _inproj(h, mx[..., :d], mx[..., d:2 * d], npre, w_in_p[l], colmajor=colmajor)

        wg = jnp.concatenate([_block_diag_halves(rg_w_r[l, 0]), _block_diag_halves(rg_w_i[l, 0]),
                              _block_diag_halves(rg_w_r[l, 1]), _block_diag_halves(rg_w_i[l, 1])],
                             axis=-1).astype(BF16)
        ya_c, ya_x = _rglru(zc, zx, conv_a_w[l], conv_a_b[l].reshape(1, dh), wg,
                            rg_b_r[l], rg_b_i[l], rg_lam[l])

        prm = jnp.zeros((SUBLANES, LANES), F32)
        prm = prm.at[0, :2 * H_B].set(gdn_a_log[l].reshape(-1)).at[1, :2 * H_B].set(gdn_dt_bias[l].reshape(-1))
        yb_c, yb_x = _gdn(zc, zx, conv_b_w[l], prm, gdn_norm[l].reshape(1, LANES))

        yc_c, yc_x = _gla(zc, zx, lb2, hg_norm[l].reshape(1, LANES), msum, mscore, l)

        h = _outproj(ya_x, yb_x, yc_x, w_out_b[l], h, mx[..., 2 * d:], npost, colmajor=colmajor)
        if l < depth - 1:
            hc = _outproj(ya_c, yb_c, yc_c, w_out_b[l], hc, mc[..., 2 * d:], npost, colmajor=False)
    return h
```

```python
import functools

import numpy as np
import jax
import jax.numpy as jnp
from jax import lax
from jax.experimental import pallas as pl
from jax.experimental.pallas import tpu as pltpu

F32 = jnp.float32
BF16 = jnp.bfloat16
HI = lax.Precision.HIGHEST

GRID_W = 64
H_A, HD_A = 8, 64
H_B, DK_B = 4, 128
H_C, DK_C = 4, 128
D_HEADS = 512
CONV_W = 4
CONV_PAD = 2
RG_C = 8.0
EPS = 1e-6

LANES = 128
SUBLANES = 8
ROW_TILE = 256
PROJ_TILE = 512
RG_SCAN_STEPS = 8
CHUNK_B = 64
TILE_C = 128
VMEM_LIMIT = 56 * 1024 * 1024

COL_XA, COL_GA, COL_Q, COL_K, COL_V, COL_GB = 0, 4, 8, 12, 16, 20
COL_QC, COL_FF, COL_FB, COL_IC, COL_GC, COL_AB = 24, 28, 32, 36, 40, 44
N_PROJ = 45 * LANES


def _dot(a, b):
    return jnp.dot(a.astype(BF16), b.astype(BF16), preferred_element_type=F32)


def _dot_nt(a, b):
    return lax.dot_general(a.astype(BF16), b.astype(BF16), (((1,), (1,)), ((), ())),
                           preferred_element_type=F32)


def _dot_tn(a, b):
    return lax.dot_general(a.astype(BF16), b.astype(BF16), (((0,), (0,)), ((), ())),
                           preferred_element_type=F32)


def _dot_hi(a, b):
    return jnp.dot(a, b, precision=HI, preferred_element_type=F32)


def _sigmoid(x):
    return 1.0 / (1.0 + jnp.exp(-x))


def _silu(x):
    return x * _sigmoid(x)


def _softplus(x):
    return jnp.maximum(x, 0.0) + jnp.log1p(jnp.exp(-jnp.abs(x)))


def _rms(x):
    return x * lax.rsqrt(jnp.mean(x * x, axis=-1, keepdims=True) + EPS)


def _params(**kw):
    return pltpu.CompilerParams(vmem_limit_bytes=VMEM_LIMIT, **kw)


def _ada_kernel(s_ref, w_ref, b_ref, o_ref):
    o_ref[...] = _dot_hi(_silu(s_ref[...]), w_ref[...]) + b_ref[...]


def _ada(cs, w_ada, b_ada):
    depth, d, n3 = w_ada.shape
    tn = 1024
    return pl.pallas_call(
        _ada_kernel,
        grid=(depth, n3 // tn),
        in_specs=[pl.BlockSpec((16, d), lambda l, j: (0, 0)),
                  pl.BlockSpec((None, d, tn), lambda l, j: (l, 0, j)),
                  pl.BlockSpec((None, 1, tn), lambda l, j: (l, 0, j))],
        out_specs=pl.BlockSpec((None, 16, tn), lambda l, j: (l, 0, j)),
        out_shape=jax.ShapeDtypeStruct((depth, 16, n3), F32),
        compiler_params=_params(),
        name="ada",
    )(cs, w_ada, b_ada.reshape(depth, 1, n3))


def _inproj_kernel(h_ref, sh_ref, sc_ref, npre_ref, w_ref, z_ref, *, colmajor, cn):
    if colmajor:
        x = jnp.concatenate([h_ref[:, j, :] for j in range(h_ref.shape[1])], axis=0)
    else:
        x = h_ref[...]
    u = (_rms(x) * npre_ref[...] * (1.0 + sc_ref[...]) + sh_ref[...]).astype(BF16)
    for j in range(z_ref.shape[-1] // cn):
        z_ref[:, j * cn:(j + 1) * cn] = jnp.dot(u, w_ref[:, j * cn:(j + 1) * cn],
                                               preferred_element_type=F32)


def _inproj(h, shift, scale, npre, w, *, colmajor):
    bsz, t_len, d = h.shape
    n = w.shape[-1]
    tc = min(PROJ_TILE, t_len)
    if colmajor:
        rows = t_len // GRID_W
        h_in = h.reshape(bsz, rows, GRID_W, d)
        h_spec = pl.BlockSpec((None, rows, tc // rows, d), lambda b, i: (b, 0, i, 0))
    else:
        h_in = h
        h_spec = pl.BlockSpec((None, tc, d), lambda b, i: (b, i, 0))
    vec = pl.BlockSpec((None, 1, d), lambda b, i: (b, 0, 0))
    return pl.pallas_call(
        functools.partial(_inproj_kernel, colmajor=colmajor, cn=1152),
        grid=(bsz, t_len // tc),
        in_specs=[h_spec, vec, vec,
                  pl.BlockSpec((1, d), lambda b, i: (0, 0)),
                  pl.BlockSpec((d, n), lambda b, i: (0, 0), pipeline_mode=pl.Buffered(1))],
        out_specs=pl.BlockSpec((None, tc, n), lambda b, i: (b, i, 0)),
        out_shape=jax.ShapeDtypeStruct((bsz, t_len, n), F32),
        compiler_params=_params(),
        name="inproj",
    )(h_in, shift, scale, npre, w)


def _outproj_kernel(ya_ref, yb_ref, yc_ref, w_ref, h_ref, gt_ref, npost_ref, o_ref, *, colmajor):
    dm = ya_ref.shape[-1]
    d = npost_ref.shape[-1]
    o = (jnp.dot(ya_ref[...].astype(BF16), w_ref[0:dm, :], preferred_element_type=F32)
         + jnp.dot(yb_ref[...].astype(BF16), w_ref[dm:2 * dm, :], preferred_element_type=F32)
         + jnp.dot(yc_ref[...].astype(BF16), w_ref[2 * dm:3 * dm, :], preferred_element_type=F32))
    upd = gt_ref[...] * (_rms(o) * npost_ref[...])
    if colmajor:
        rows = h_ref.shape[0]
        for j in range(h_ref.shape[1]):
            o_ref[:, j, :] = h_ref[:, j, :] + upd[j * rows:(j + 1) * rows, :]
    else:
        o_ref[...] = h_ref[...] + upd


def _outproj(ya, yb, yc, w, h, gate, npost, *, colmajor):
    bsz, t_len, d = h.shape
    dm = ya.shape[-1]
    tc = min(PROJ_TILE, t_len)
    if colmajor:
        rows = t_len // GRID_W
        h_in = h.reshape(bsz, rows, GRID_W, d)
        h_spec = pl.BlockSpec((None, rows, tc // rows, d), lambda b, i: (b, 0, i, 0))
    else:
        h_in = h
        h_spec = pl.BlockSpec((None, tc, d), lambda b, i: (b, i, 0))
    y_spec = pl.BlockSpec((None, tc, dm), lambda b, i: (b, i, 0))
    out = pl.pallas_call(
        functools.partial(_outproj_kernel, colmajor=colmajor),
        grid=(bsz, t_len // tc),
        in_specs=[y_spec, y_spec, y_spec,
                  pl.BlockSpec((3 * dm, d), lambda b, i: (0, 0)),
                  h_spec,
                  pl.BlockSpec((None, 1, d), lambda b, i: (b, 0, 0)),
                  pl.BlockSpec((1, d), lambda b, i: (0, 0))],
        out_specs=h_spec,
        out_shape=jax.ShapeDtypeStruct(h_in.shape, F32),
        compiler_params=_params(),
        name="outproj",
    )(ya, yb, yc, w, h_in, gate, npost)
    return out.reshape(bsz, t_len, d)


def _conv_tile(pad_ref, cw_ref, i):
    base = i * ROW_TILE + SUBLANES - CONV_PAD
    acc = cw_ref[0:1, :] * pad_ref[base:base + ROW_TILE, :]
    for j in range(1, CONV_W):
        acc = acc + cw_ref[j:j + 1, :] * pad_ref[base + j:base + j + ROW_TILE, :]
    return acc


def _fill_pad(pad_ref, x_ref, t_len):
    zeros = jnp.zeros((SUBLANES, pad_ref.shape[-1]), F32)
    pad_ref[0:SUBLANES, :] = zeros
    pad_ref[SUBLANES:SUBLANES + t_len, :] = x_ref[...]
    pad_ref[SUBLANES + t_len:2 * SUBLANES + t_len, :] = zeros


def _rglru_kernel(xc_ref, gc_ref, xx_ref, gx_ref, cw_ref, cb_ref, wg_ref, br_ref, bi_ref, lam_ref,
                  oc_ref, ox_ref, pad_ref, h_s):
    dh = xc_ref.shape[-1]
    half = dh // 2
    nv = RG_SCAN_STEPS
    sub_rows = nv * SUBLANES
    n_sub = ROW_TILE // sub_rows
    sp = _softplus(-lam_ref[...])
    sub = lax.broadcasted_iota(jnp.int32, (SUBLANES, dh), 0)

    nlb = dh // LANES

    def gates(i, dirn):
        vr = []
        for b in range(n_sub):
            base = i * ROW_TILE + b * sub_rows + SUBLANES - CONV_PAD
            vr.append([jnp.concatenate([pad_ref[lb, pl.ds(base + j, SUBLANES, stride=nv), :]
                                        for lb in range(nlb)], axis=1) for j in range(nv + CONV_W - 1)])
        xc = cb_ref[...]
        for tap in range(CONV_W):
            xc = xc + cw_ref[tap:tap + 1, :] * jnp.concatenate(
                [vr[b][tap + j] for b in range(n_sub) for j in range(nv)], axis=0)
        pre = [_dot(xc[:, hf * half:(hf + 1) * half], wg_ref[hf, :, dirn * dh:(dirn + 1) * dh])
               for hf in range(2)]
        r_pre = jnp.concatenate([pre[0][:, :half], pre[1][:, :half]], axis=1)
        i_pre = jnp.concatenate([pre[0][:, half:], pre[1][:, half:]], axis=1)
        r = _sigmoid(r_pre + br_ref[dirn:dirn + 1, :])
        ig = _sigmoid(i_pre + bi_ref[dirn:dirn + 1, :])
        log_a = -RG_C * r * sp[dirn:dirn + 1, :]
        a = jnp.exp(log_a)
        y = -jnp.tanh(log_a) * (a * a + 1.0)
        u = jnp.where(y > 0.0, y * lax.rsqrt(y), 0.0) * (ig * xc)
        return a, u

    def tile_scan(a, u, carry, rev):
        out = [None] * n_sub
        for b in (range(n_sub)[::-1] if rev else range(n_sub)):
            rows = slice(b * sub_rows, (b + 1) * sub_rows)
            out[b], carry = sub_scan(a[rows, :], u[rows, :], carry, rev)
        return [h for hs in out for h in hs], carry

    def sub_scan(a, u, carry, rev):
        order = list(range(nv))[::-1] if rev else list(range(nv))
        a_cum, h_loc = [None] * nv, [None] * nv
        prev = None
        for j in order:
            aj, uj = a[j * SUBLANES:(j + 1) * SUBLANES, :], u[j * SUBLANES:(j + 1) * SUBLANES, :]
            if prev is None:
                a_cum[j], h_loc[j] = aj, uj
            else:
                a_cum[j], h_loc[j] = aj * a_cum[prev], aj * h_loc[prev] + uj
            prev = j
        p, r = a_cum[prev], h_loc[prev]
        for sft in (1, 2, 4):
            keep = (sub < SUBLANES - sft) if rev else (sub >= sft)
            amt = SUBLANES - sft if rev else sft
            r = r + p * jnp.where(keep, pltpu.roll(r, amt, 0), 0.0)
            p = p * jnp.where(keep, pltpu.roll(p, amt, 0), 1.0)
        state = p * carry + r
        keep = (sub < SUBLANES - 1) if rev else (sub >= 1)
        c_in = jnp.where(keep, pltpu.roll(state, SUBLANES - 1 if rev else 1, 0), carry)
        last = 0 if rev else SUBLANES - 1
        return [h_loc[j] + a_cum[j] * c_in for j in range(nv)], state[last:last + 1, :]

    def seg(x_ref, g_ref, o_ref, t_len, h0):
        n_tiles = t_len // ROW_TILE
        zeros = jnp.zeros((SUBLANES, LANES), F32)
        for lb in range(nlb):
            pad_ref[lb, 0:SUBLANES, :] = zeros
            pad_ref[lb, SUBLANES:SUBLANES + t_len, :] = x_ref[:, lb * LANES:(lb + 1) * LANES]
            pad_ref[lb, SUBLANES + t_len:2 * SUBLANES + t_len, :] = zeros
        carries = []
        for dirn in range(2):
            def tile_body(it, carry, dirn=dirn):
                i = it if dirn == 0 else n_tiles - 1 - it
                a, u = gates(i, dirn)
                hs, carry = tile_scan(a, u, carry, dirn == 1)
                for bj in range(n_sub * nv):
                    start = i * ROW_TILE + (bj // nv) * sub_rows + bj % nv
                    for lb in range(nlb):
                        h_s[dirn, lb, pl.ds(start, SUBLANES, stride=nv), :] = (
                            hs[bj][:, lb * LANES:(lb + 1) * LANES])
                return carry

            carries.append(lax.fori_loop(0, n_tiles, tile_body, h0[dirn]))
        for i in range(n_tiles):
            rows = slice(i * ROW_TILE, (i + 1) * ROW_TILE)
            hsum = jnp.concatenate([h_s[0, lb, rows, :] + h_s[1, lb, rows, :] for lb in range(nlb)], axis=1)
            o_ref[rows, :] = hsum * _silu(g_ref[rows, :])
        return tuple(carries)

    zero = jnp.zeros((1, dh), F32)
    states = seg(xc_ref, gc_ref, oc_ref, xc_ref.shape[0], (zero, zero))
    seg(xx_ref, gx_ref, ox_ref, xx_ref.shape[0], states)


def _rglru(zc, zx, cw, cb, wg, br, bi, lam):
    bsz, tc_len, _ = zc.shape
    tx_len = zx.shape[1]
    dh = D_HEADS
    nb = dh // LANES

    def zspec(t_len, col):
        return pl.BlockSpec((None, t_len, dh), lambda b: (b, 0, col // nb))

    def full(a):
        return pl.BlockSpec(a.shape, lambda b: (0,) * a.ndim)

    return pl.pallas_call(
        _rglru_kernel,
        grid=(bsz,),
        in_specs=[zspec(tc_len, COL_XA), zspec(tc_len, COL_GA), zspec(tx_len, COL_XA), zspec(tx_len, COL_GA),
                  full(cw), full(cb), full(wg), full(br), full(bi), full(lam)],
        out_specs=[pl.BlockSpec((None, tc_len, dh), lambda b: (b, 0, 0)),
                   pl.BlockSpec((None, tx_len, dh), lambda b: (b, 0, 0))],
        out_shape=[jax.ShapeDtypeStruct((bsz, tc_len, dh), F32),
                   jax.ShapeDtypeStruct((bsz, tx_len, dh), F32)],
        scratch_shapes=[pltpu.VMEM((nb, tx_len + 2 * SUBLANES, LANES), F32),
                        pltpu.VMEM((2, nb, tx_len, LANES), F32)],
        compiler_params=_params(),
        name="rglru",
    )(zc, zc, zx, zx, cw, cb, wg, br, bi, lam)


GDN_UNROLL = 8
GDN_SPLICES = 4


def _gdn_masks():
    c = CHUNK_B
    i = np.arange(c)[:, None]
    k = np.arange(c)[None, :]
    stacked = np.stack([np.concatenate([k <= i, k > i], 0), np.concatenate([k >= i, k < i], 0)])
    return np.tile(stacked, (1, 1, 3)).astype(np.float32)


def _split3_rows(x):
    hi = x.astype(BF16)
    r1 = x - hi.astype(F32)
    mid = r1.astype(BF16)
    lo = (r1 - mid.astype(F32)).astype(BF16)
    return jnp.concatenate([hi, mid, lo], axis=0)


INV_BLOCK = 16


def _gdn_kernel(qc_ref, kc_ref, vc_ref, abc_ref, gtc_ref, qx_ref, kx_ref, vx_ref, abx_ref, gtx_ref,
                cwq_ref, cwk_ref, cwv_ref, prm_ref, nw_ref, mst_ref, oc_ref, ox_ref,
                pad_ref, q_s, k_s, v_s, gf_s, bf_s, gb_s, bb_s, o_s, qp_s, p_s, n_s, et_s,
                o_c, qp_c, p_c, n_c, et_c):
    head = pl.program_id(1)
    c = CHUNK_B
    ii = lax.broadcasted_iota(jnp.int32, (c, c), 0)
    jj = lax.broadcasted_iota(jnp.int32, (c, c), 1)
    eye = (ii == jj).astype(F32)
    a_mask = (ii > jj, ii < jj)
    s_mask = (ii >= jj, ii <= jj)
    d_mask = (ii // INV_BLOCK) == (jj // INV_BLOCK)
    f_mask = []
    size = INV_BLOCK
    while size < c:
        same = (ii // (2 * size)) == (jj // (2 * size))
        hi_i, hi_j = (ii // size) % 2, (jj // size) % 2
        f_mask.append((same & (hi_i == 1) & (hi_j == 0), same & (hi_i == 0) & (hi_j == 1)))
        size *= 2
    sel_r = lax.broadcasted_iota(jnp.int32, (3 * LANES, 4 * LANES), 0) % LANES
    sel_b = lax.broadcasted_iota(jnp.int32, (3 * LANES, 4 * LANES), 1) // LANES
    sel4 = (sel_r == head + H_B * (2 * (sel_b % 2) + sel_b // 2)).astype(BF16)
    lane = lax.broadcasted_iota(jnp.int32, (ROW_TILE, LANES), 1)

    def prep(q_ref, k_ref, v_ref, ab_ref, t_len):
        n_tiles = t_len // ROW_TILE
        for src, dst, cw_ref, kind in ((q_ref, q_s, cwq_ref, "q"), (k_ref, k_s, cwk_ref, "k"),
                                       (v_ref, v_s, cwv_ref, "v")):
            _fill_pad(pad_ref, src, t_len)
            for i in range(n_tiles):
                xc = _silu(_conv_tile(pad_ref, cw_ref, i))
                if kind != "v":
                    xc = xc * lax.rsqrt(jnp.sum(xc * xc, axis=-1, keepdims=True) + EPS)
                if kind == "q":
                    xc = xc * (DK_B ** -0.5)
                dst[i * ROW_TILE:(i + 1) * ROW_TILE, :] = xc
        for i in range(n_tiles):
            rows = slice(i * ROW_TILE, (i + 1) * ROW_TILE)
            ab = ab_ref[rows, :]
            gval = -jnp.exp(prm_ref[0:1, :]) * _softplus(ab + prm_ref[1:2, :])
            gbv = jnp.where(lane < 2 * H_B, gval, _sigmoid(ab))
            hi = gbv.astype(BF16)
            r1 = gbv - hi.astype(F32)
            mid = r1.astype(BF16)
            lo = (r1 - mid.astype(F32)).astype(BF16)
            bc = jnp.dot(jnp.concatenate([hi, mid, lo], axis=1), sel4, preferred_element_type=F32)
            for j, dst in enumerate((gf_s, bf_s, gb_s, bb_s)):
                dst[rows, :] = bc[:, j * LANES:(j + 1) * LANES]

    def chunk_terms(items, scr, hooks=()):
        o_s, qp_s, p_s, n_s, et_s = scr
        hooks = list(hooks)
        per_splice = -(-len(hooks) // GDN_SPLICES)

        def splice():
            for _ in range(min(per_splice, len(hooks))):
                hooks.pop(0)()

        m = range(len(items))
        dirs = [d for d, _ in items]
        rows = [pl.ds(pl.multiple_of(ci * c, c), c) for _, ci in items]
        prow = [pl.ds(pl.multiple_of(ci * LANES, LANES), LANES) for _, ci in items]
        erow = [pl.ds(pl.multiple_of(ci * SUBLANES, SUBLANES), SUBLANES) for _, ci in items]
        q = [q_s[r, :] for r in rows]
        k = [k_s[r, :] for r in rows]
        v = [v_s[r, :] for r in rows]
        g = [(gf_s, gb_s)[d][r, :] for d, r in zip(dirs, rows)]
        beta = [(bf_s, bb_s)[d][r, :] for d, r in zip(dirs, rows)]
        cum = [jnp.dot(mst_ref[dirs[i]], _split3_rows(g[i]), preferred_element_type=F32) for i in m]
        kb = [k[i] * beta[i] for i in m]
        kq = [_dot_nt(jnp.concatenate([kb[i], q[i]], axis=0), k[i]) for i in m]
        splice()
        e_in = [jnp.exp(cum[i][0:c]) for i in m]
        e_rest = [jnp.exp(cum[i][c:2 * c]) for i in m]
        cs = [cum[i][0:c, 0:c] for i in m]
        dec = [jnp.exp(jnp.where(s_mask[dirs[i]], cs[i] - cs[i].T, 0.0)) for i in m]
        for i in m:
            et_s[dirs[i], erow[i], :] = jnp.exp((cum[i][0:SUBLANES] + cum[i][c:c + SUBLANES]))
        a = [jnp.where(a_mask[dirs[i]], kq[i][:c] * dec[i], 0.0) for i in m]
        qk = [jnp.where(s_mask[dirs[i]], kq[i][c:] * dec[i], 0.0) for i in m]
        dg = [jnp.where(d_mask, a[i], 0.0) for i in m]
        x1 = [_dot(dg[i], dg[i]) for i in m]
        x2 = [_dot(x1[i], x1[i]) for i in m]
        splice()
        st = [_dot(jnp.concatenate([x1[i], x2[i]], axis=0), x2[i]) for i in m]
        q2 = [x1[i] + x2[i] + st[i][:c] for i in m]
        q3 = [q2[i] + st[i][c:] + _dot(q2[i], st[i][c:]) for i in m]
        imd = [eye - dg[i] for i in m]
        t_inv = [imd[i] + _dot(imd[i], q3[i]) for i in m]
        splice()
        for fm in f_mask:
            tf = [_dot(t_inv[i], jnp.where(fm[dirs[i]], a[i], 0.0)) for i in m]
            t_inv = [t_inv[i] - _dot(tf[i], t_inv[i]) for i in m]
        splice()
        uw = [_dot(t_inv[i], jnp.concatenate([v[i] * beta[i], kb[i] * e_in[i]], axis=1)) for i in m]
        res = [_dot(jnp.concatenate([qk[i], (k[i] * e_rest[i]).T], axis=0), uw[i]) for i in m]
        for i in m:
            n_s[dirs[i], prow[i], :] = res[i][c:, :LANES]
            p_s[dirs[i], prow[i], :] = res[i][c:, LANES:].astype(BF16)
            o_s[dirs[i], rows[i], :] = res[i][:c, :LANES]
            qp_s[dirs[i], rows[i], :] = (q[i] * e_in[i] - res[i][:c, LANES:]).astype(BF16)

    state = [(jnp.zeros((LANES, LANES), F32),) * 2]

    def seg(q_ref, k_ref, v_ref, ab_ref, gt_ref, o_ref, scr, pending):
        o_s, qp_s, p_s, n_s, et_s = scr
        t_len = q_ref.shape[0]
        n_chunks = t_len // c
        prep(q_ref, k_ref, v_ref, ab_ref, t_len)

        unroll = min(GDN_UNROLL, n_chunks)
        n_full = n_chunks // unroll

        def terms(start, size, hooks):
            chunk_terms([(dirn, start + j if dirn == 0 else n_chunks - 1 - (start + j))
                         for j in range(size) for dirn in range(2)], scr, hooks)

        def state_hooks(start, size):
            def hook(n):
                def run():
                    st, new = state[0], []
                    for dirn, ci in ((0, n), (1, n_chunks - 1 - n)):
                        rows = pl.ds(pl.multiple_of(ci * c, c), c)
                        prow = pl.ds(pl.multiple_of(ci * LANES, LANES), LANES)
                        sb = st[dirn].astype(BF16)
                        o_s[dirn, rows, :] = o_s[dirn, rows, :] + jnp.dot(qp_s[dirn, rows, :], sb,
                                                                         preferred_element_type=F32)
                        e_tot = et_s[dirn, pl.ds(pl.multiple_of(ci * SUBLANES, SUBLANES), 1), :]
                        new.append(st[dirn] * e_tot + n_s[dirn, prow, :]
                                   - jnp.dot(p_s[dirn, prow, :], sb, preferred_element_type=F32))
                    state[0] = tuple(new)
                return run
            return [hook(start + j) for j in range(size)]

        def finish():
            for i in range(t_len // ROW_TILE):
                rows = slice(i * ROW_TILE, (i + 1) * ROW_TILE)
                o = o_s[0, rows, :] + o_s[1, rows, :]
                o_ref[rows, :] = _rms(o) * nw_ref[...] * _silu(gt_ref[rows, :])

        terms(0, unroll, pending)

        def pass_body(p, st):
            state[0] = st
            terms(p * unroll, unroll, state_hooks((p - 1) * unroll, unroll))
            return state[0]

        state[0] = lax.fori_loop(1, n_full, pass_body, state[0])
        return state_hooks((n_full - 1) * unroll, unroll) + [finish]

    pending = seg(qc_ref, kc_ref, vc_ref, abc_ref, gtc_ref, oc_ref, (o_c, qp_c, p_c, n_c, et_c), [])
    for run in seg(qx_ref, kx_ref, vx_ref, abx_ref, gtx_ref, ox_ref, (o_s, qp_s, p_s, n_s, et_s), pending):
        run()


def _gdn(zc, zx, cwb, prm, nw):
    bsz, tc_len, _ = zc.shape
    tx_len = zx.shape[1]

    def zspec(t_len, col, per_head=True):
        if per_head:
            return pl.BlockSpec((None, t_len, LANES), lambda b, h: (b, 0, col + h))
        return pl.BlockSpec((None, t_len, LANES), lambda b, h: (b, 0, col))

    def seg_specs(t_len):
        return [zspec(t_len, COL_Q), zspec(t_len, COL_K), zspec(t_len, COL_V),
                zspec(t_len, COL_AB, False), zspec(t_len, COL_GB)]

    def cw_spec(off):
        return pl.BlockSpec((CONV_W, LANES), lambda b, h: (0, off + h))

    mst = jnp.asarray(_gdn_masks(), BF16)

    def terms_scratch(t_len):
        n_chunks = t_len // CHUNK_B
        return [pltpu.VMEM((2, t_len, LANES), F32),
                pltpu.VMEM((2, t_len, LANES), BF16),
                pltpu.VMEM((2, n_chunks * LANES, LANES), BF16),
                pltpu.VMEM((2, n_chunks * LANES, LANES), F32),
                pltpu.VMEM((2, n_chunks * SUBLANES, LANES), F32)]

    return pl.pallas_call(
        _gdn_kernel,
        grid=(bsz, H_B),
        in_specs=seg_specs(tc_len) + seg_specs(tx_len) + [
            cw_spec(0), cw_spec(H_B), cw_spec(2 * H_B),
            pl.BlockSpec(prm.shape, lambda b, h: (0, 0)),
            pl.BlockSpec(nw.shape, lambda b, h: (0, 0)),
            pl.BlockSpec(mst.shape, lambda b, h: (0, 0, 0))],
        out_specs=[pl.BlockSpec((None, tc_len, LANES), lambda b, h: (b, 0, h)),
                   pl.BlockSpec((None, tx_len, LANES), lambda b, h: (b, 0, h))],
        out_shape=[jax.ShapeDtypeStruct((bsz, tc_len, D_HEADS), F32),
                   jax.ShapeDtypeStruct((bsz, tx_len, D_HEADS), F32)],
        scratch_shapes=[pltpu.VMEM((tx_len + 2 * SUBLANES, LANES), F32)]
                       + [pltpu.VMEM((tx_len, LANES), F32) for _ in range(7)]
                       + terms_scratch(tx_len) + terms_scratch(tc_len),
        compiler_params=_params(),
        name="gdn",
    )(zc, zc, zc, zc, zc, zx, zx, zx, zx, zx, cwb, cwb, cwb, prm, nw, mst)


_LEVELS = (1, 2, 4, 8, 16, 32, 64)


def _gla_masks():
    n = TILE_C
    i = np.arange(n)[:, None]
    k = np.arange(n)[None, :]
    sums, scores = [], []
    for lv in _LEVELS:
        same = (i // (2 * lv)) == (k // (2 * lv))
        hi_i, hi_k = (i // lv) % 2, (k // lv) % 2
        if lv < SUBLANES:
            sums.append(same & np.where(hi_i == 1, (hi_k == 1) & (k <= i), (hi_k == 0) & (k > i)))
        scores.append(same & (hi_i == 1) & (hi_k == 0))
    sums += [k <= i, k > i]
    scores.append(i == k)
    sums_f = np.stack(sums).astype(np.float32)
    scores_f = np.stack(scores).astype(np.float32)
    sums_all = np.stack([sums_f, sums_f[:, ::-1, ::-1]]).reshape(2, len(sums) * n, n)
    scores_all = np.stack([scores_f, scores_f[:, ::-1, ::-1]])
    return np.tile(sums_all, (1, 1, 2)), scores_all


GLA_UNROLL = 4


def _gla_kernel(qc_ref, ffc_ref, fbc_ref, ic_ref, gtc_ref, qx_ref, ffx_ref, fbx_ref, ix_ref, gtx_ref,
                lb_ref, nw_ref, msum_ref, mscore_ref, oc_ref, ox_ref, oi_s, qd_s, u_s, et_s, *, layer):
    n = TILE_C
    n_lv = len(_LEVELS)
    depth = lb_ref.shape[0] // 2
    lbs = []
    for dirn in range(2):
        rows = [lb_ref[2 * j + dirn:2 * j + dirn + 1, :] for j in range(depth)]
        mx = functools.reduce(jnp.maximum, rows)
        ex = [jnp.exp(r - mx) for r in rows]
        den = functools.reduce(lambda p, q: p + q, ex)
        acc = jnp.zeros_like(mx)
        for j in range(1, layer + 1):
            acc = acc + ex[j] / den
        lbs.append(acc)

    def tile_terms(items, q_ref, f_refs, i_ref):
        m = range(len(items))
        dirs = [d for d, _ in items]
        rows = [pl.ds(pl.multiple_of(ti * n, n), n) for _, ti in items]
        q = [_silu(q_ref[r, :]) for r in rows]
        v = [i_ref[r, :] for r in rows]
        fg = [lbs[dirs[i]] + (1.0 - lbs[dirs[i]]) * _sigmoid(f_refs[dirs[i]][rows[i], :]) for i in m]
        k = [1.0 - fg[i] for i in m]
        lf = [jnp.log(fg[i]) for i in m]
        hi = [lf[i].astype(BF16) for i in m]
        lf2 = [jnp.concatenate([hi[i], (lf[i] - hi[i].astype(F32)).astype(BF16)], axis=0) for i in m]

        sums = [jnp.dot(msum_ref[dirs[i]], lf2[i], preferred_element_type=F32) for i in m]

        def csum(j):
            return [sums[i][j * n:(j + 1) * n, :] for i in m]

        n_small = sum(1 for lv in _LEVELS if lv < SUBLANES)
        c_in, c_rest = csum(n_small), csum(n_small + 1)

        def boundary_exponent(i, lv):
            off = lv - 1 if dirs[i] == 0 else lv
            ref = jnp.concatenate(
                [jnp.broadcast_to(c_in[i][blk * 2 * lv + off:blk * 2 * lv + off + 1, :], (2 * lv, LANES))
                 for blk in range(n // (2 * lv))], axis=0)
            return -jnp.abs(c_in[i] - ref)

        qb = [q[i].astype(BF16) for i in m]
        kb = [k[i].astype(BF16) for i in m]
        scores = [mscore_ref[dirs[i], n_lv] * _dot_nt(qb[i], kb[i]).astype(BF16) for i in m]
        for j, lv in enumerate(_LEVELS):
            xs = csum(j) if lv < SUBLANES else [boundary_exponent(i, lv) for i in m]
            e = [jnp.exp(x).astype(BF16) for x in xs]
            p = [_dot_nt(qb[i] * e[i], kb[i] * e[i]) for i in m]
            scores = [scores[i] + mscore_ref[dirs[i], j] * p[i].astype(BF16) for i in m]
        o_intra = [_dot(scores[i], v[i]) for i in m]
        u = [_dot_tn(v[i], k[i] * jnp.exp(c_rest[i])) for i in m]
        for i, (_, ti) in enumerate(items):
            oi_s[dirs[i], rows[i], :] = o_intra[i]
            qd_s[dirs[i], rows[i], :] = (q[i] * jnp.exp(c_in[i])).astype(BF16)
            u_s[dirs[i], rows[i], :] = u[i]
            et_s[dirs[i], pl.ds(pl.multiple_of(ti * SUBLANES, SUBLANES), SUBLANES), :] = jnp.exp(
                (c_in[i] + c_rest[i])[0:SUBLANES, :])

    def seg(q_ref, ff_ref, fb_ref, i_ref, gt_ref, o_ref, states):
        t_len = q_ref.shape[0]
        n_tiles = t_len // n

        unroll = min(GLA_UNROLL, n_tiles)
        n_pass = n_tiles // unroll

        def terms(p):
            tile_terms([(dirn, p * unroll + j if dirn == 0 else n_tiles - 1 - (p * unroll + j))
                        for j in range(unroll) for dirn in range(2)], q_ref, (ff_ref, fb_ref), i_ref)

        def state_steps(p, st):
            for j in range(unroll):
                ti = p * unroll + j
                new = []
                for dirn, tj in ((0, ti), (1, n_tiles - 1 - ti)):
                    rows = pl.ds(pl.multiple_of(tj * n, n), n)
                    oi_s[dirn, rows, :] = oi_s[dirn, rows, :] + lax.dot_general(
                        qd_s[dirn, rows, :], st[dirn].astype(BF16), (((1,), (1,)), ((), ())),
                        preferred_element_type=F32)
                    e_tot = et_s[dirn, pl.ds(pl.multiple_of(tj * SUBLANES, SUBLANES), 1), :]
                    new.append(st[dirn] * e_tot + u_s[dirn, rows, :])
                st = tuple(new)
            return st

        terms(0)

        def pass_body(p, st):
            st = state_steps(p - 1, st)
            terms(p)
            return st

        states = lax.fori_loop(1, n_pass, pass_body, states)
        states = state_steps(n_pass - 1, states)
        for i in range(t_len // ROW_TILE):
            rows = slice(i * ROW_TILE, (i + 1) * ROW_TILE)
            o = oi_s[0, rows, :] + oi_s[1, rows, :]
            o_ref[rows, :] = _rms(o) * nw_ref[...] * _silu(gt_ref[rows, :])
        return states

    zero = jnp.zeros((LANES, LANES), F32)
    states = seg(qc_ref, ffc_ref, fbc_ref, ic_ref, gtc_ref, oc_ref, (zero, zero))
    seg(qx_ref, ffx_ref, fbx_ref, ix_ref, gtx_ref, ox_ref, states)


def _gla(zc, zx, lb, nw, msum, mscore, layer):
    bsz, tc_len, _ = zc.shape
    tx_len = zx.shape[1]

    def zspec(t_len, col):
        return pl.BlockSpec((None, t_len, LANES), lambda b, h: (b, 0, col + h))

    def seg_specs(t_len):
        return [zspec(t_len, COL_QC), zspec(t_len, COL_FF), zspec(t_len, COL_FB),
                zspec(t_len, COL_IC), zspec(t_len, COL_GC)]

    return pl.pallas_call(
        functools.partial(_gla_kernel, layer=layer),
        grid=(bsz, H_C),
        in_specs=seg_specs(tc_len) + seg_specs(tx_len) + [
            pl.BlockSpec((lb.shape[0], LANES), lambda b, h: (0, h)),
            pl.BlockSpec(nw.shape, lambda b, h: (0, 0)),
            pl.BlockSpec(msum.shape, lambda b, h: (0, 0, 0)),
            pl.BlockSpec(mscore.shape, lambda b, h: (0, 0, 0, 0))],
        out_specs=[pl.BlockSpec((None, tc_len, LANES), lambda b, h: (b, 0, h)),
                   pl.BlockSpec((None, tx_len, LANES), lambda b, h: (b, 0, h))],
        out_shape=[jax.ShapeDtypeStruct((bsz, tc_len, D_HEADS), F32),
                   jax.ShapeDtypeStruct((bsz, tx_len, D_HEADS), F32)],
        scratch_shapes=[pltpu.VMEM((2, tx_len, LANES), F32),
                        pltpu.VMEM((2, tx_len, LANES), BF16),
                        pltpu.VMEM((2, tx_len, LANES), F32),
                        pltpu.VMEM((2, (tx_len // TILE_C) * SUBLANES, LANES), F32)],
        compiler_params=_params(),
        name="hgrn2",
    )(zc, zc, zc, zc, zc, zx, zx, zx, zx, zx, lb, nw, msum, mscore)


def _block_diag_halves(w):
    w4 = w.reshape(2, H_A // 2, HD_A, HD_A)
    eye = jnp.eye(H_A // 2, dtype=w.dtype)
    return jnp.einsum("ghij,hk->ghikj", w4, eye).reshape(2, (H_A // 2) * HD_A, (H_A // 2) * HD_A)


def kernel(x, c, ctx, c_ctx, w_ada, b_ada, norm_pre, norm_post, w_in, conv_a_w, conv_a_b, rg_w_r, rg_b_r,
           rg_w_i, rg_b_i, rg_lam, conv_b_w, gdn_a_log, gdn_dt_bias, gdn_norm, hg_lb, hg_norm, w_out):
    bsz, t_len, d = x.shape
    depth = w_ada.shape[0]
    dh = D_HEADS

    cs = jnp.concatenate([c, c_ctx[None, :], jnp.zeros((16 - bsz - 1, d), F32)], axis=0)
    mod = _ada(cs, w_ada, b_ada)

    qkv_end = 2 * dh + 3 * dh
    w_in_b = w_in.astype(BF16)
    w_in_p = jnp.concatenate(
        [w_in_b[:, :, :qkv_end], w_in_b[:, :, qkv_end + 4 * H_B:], w_in_b[:, :, qkv_end:qkv_end + 4 * H_B],
         jnp.zeros((depth, d, LANES - 4 * H_B), BF16)], axis=-1)
    w_out_b = w_out.astype(BF16)
    msum_np, mscore_np = _gla_masks()
    msum = jnp.asarray(msum_np, BF16)
    mscore = jnp.asarray(mscore_np, BF16)
    lb2 = hg_lb.reshape(depth * 2, dh)

    h, hc = x, ctx
    for l in range(depth):
        colmajor = l % 2 == 1
        mx = mod[l, :bsz].reshape(bsz, 1, 3 * d)
        mc = jnp.broadcast_to(mod[l, bsz].reshape(1, 1, 3 * d), (bsz, 1, 3 * d))
        npre = norm_pre[l].reshape(1, d)
        npost = norm_post[l].reshape(1, d)
        zc = _inproj(hc, mc[..., :d], mc[..., d:2 * d], npre, w_in_p[l], colmajor=False)
        zx = _inproj(h, mx[..., :d], mx[..., d:2 * d], npre, w_in_p[l], colmajor=colmajor)

        wg = jnp.concatenate([_block_diag_halves(rg_w_r[l, 0]), _block_diag_halves(rg_w_i[l, 0]),
                              _block_diag_halves(rg_w_r[l, 1]), _block_diag_halves(rg_w_i[l, 1])],
                             axis=-1).astype(BF16)
        ya_c, ya_x = _rglru(zc, zx, conv_a_w[l], conv_a_b[l].reshape(1, dh), wg,
                            rg_b_r[l], rg_b_i[l], rg_lam[l])

        prm = jnp.zeros((SUBLANES, LANES), F32)
        prm = prm.at[0, :2 * H_B].set(gdn_a_log[l].reshape(-1)).at[1, :2 * H_B].set(gdn_dt_bias[l].reshape(-1))
        yb_c, yb_x = _gdn(zc, zx, conv_b_w[l], prm, gdn_norm[l].reshape(1, LANES))

        yc_c, yc_x = _gla(zc, zx, lb2, hg_norm[l].reshape(1, LANES), msum, mscore, l)

        h = _outproj(ya_x, yb_x, yc_x, w_out_b[l], h, mx[..., 2 * d:], npost, colmajor=colmajor)
        if l < depth - 1:
            hc = _outproj(ya_c, yb_c, yc_c, w_out_b[l], hc, mc[..., 2 * d:], npost, colmajor=False)
    return h
```

```python
import functools

import numpy as np
import jax
import jax.numpy as jnp
from jax import lax
from jax.experimental import pallas as pl
from jax.experimental.pallas import tpu as pltpu

F32 = jnp.float32
BF16 = jnp.bfloat16
HI = lax.Precision.HIGHEST

GRID_W = 64
H_A, HD_A = 8, 64
H_B, DK_B = 4, 128
H_C, DK_C = 4, 128
D_HEADS = 512
CONV_W = 4
CONV_PAD = 2
RG_C = 8.0
EPS = 1e-6

LANES = 128
SUBLANES = 8
ROW_TILE = 256
PROJ_TILE = 512
RG_SCAN_STEPS = 8
CHUNK_B = 64
TILE_C = 128
VMEM_LIMIT = 56 * 1024 * 1024

COL_XA, COL_GA, COL_Q, COL_K, COL_V, COL_GB = 0, 4, 8, 12, 16, 20
COL_QC, COL_FF, COL_FB, COL_IC, COL_GC, COL_AB = 24, 28, 32, 36, 40, 44
N_PROJ = 45 * LANES


def _dot(a, b):
    return jnp.dot(a.astype(BF16), b.astype(BF16), preferred_element_type=F32)


def _dot_nt(a, b):
    return lax.dot_general(a.astype(BF16), b.astype(BF16), (((1,), (1,)), ((), ())),
                           preferred_element_type=F32)


def _dot_tn(a, b):
    return lax.dot_general(a.astype(BF16), b.astype(BF16), (((0,), (0,)), ((), ())),
                           preferred_element_type=F32)


def _dot_hi(a, b):
    return jnp.dot(a, b, precision=HI, preferred_element_type=F32)


def _sigmoid(x):
    return 1.0 / (1.0 + jnp.exp(-x))


def _silu(x):
    return x * _sigmoid(x)


def _softplus(x):
    return jnp.maximum(x, 0.0) + jnp.log1p(jnp.exp(-jnp.abs(x)))


def _rms(x):
    return x * lax.rsqrt(jnp.mean(x * x, axis=-1, keepdims=True) + EPS)


def _params(**kw):
    return pltpu.CompilerParams(vmem_limit_bytes=VMEM_LIMIT, **kw)


def _ada_kernel(s_ref, w_ref, b_ref, o_ref):
    o_ref[...] = _dot_hi(_silu(s_ref[...]), w_ref[...]) + b_ref[...]


def _ada(cs, w_ada, b_ada):
    depth, d, n3 = w_ada.shape
    tn = 1024
    return pl.pallas_call(
        _ada_kernel,
        grid=(depth, n3 // tn),
        in_specs=[pl.BlockSpec((16, d), lambda l, j: (0, 0)),
                  pl.BlockSpec((None, d, tn), lambda l, j: (l, 0, j)),
                  pl.BlockSpec((None, 1, tn), lambda l, j: (l, 0, j))],
        out_specs=pl.BlockSpec((None, 16, tn), lambda l, j: (l, 0, j)),
        out_shape=jax.ShapeDtypeStruct((depth, 16, n3), F32),
        compiler_params=_params(),
        name="ada",
    )(cs, w_ada, b_ada.reshape(depth, 1, n3))


def _inproj_kernel(h_ref, sh_ref, sc_ref, npre_ref, w_ref, z_ref, *, colmajor, cn):
    if colmajor:
        x = jnp.concatenate([h_ref[:, j, :] for j in range(h_ref.shape[1])], axis=0)
    else:
        x = h_ref[...]
    u = (_rms(x) * npre_ref[...] * (1.0 + sc_ref[...]) + sh_ref[...]).astype(BF16)
    for j in range(z_ref.shape[-1] // cn):
        z_ref[:, j * cn:(j + 1) * cn] = jnp.dot(u, w_ref[:, j * cn:(j + 1) * cn],
                                               preferred_element_type=F32)


def _inproj(h, shift, scale, npre, w, *, colmajor):
    bsz, t_len, d = h.shape
    n = w.shape[-1]
    tc = min(PROJ_TILE, t_len)
    if colmajor:
        rows = t_len // GRID_W
        h_in = h.reshape(bsz, rows, GRID_W, d)
        h_spec = pl.BlockSpec((None, rows, tc // rows, d), lambda b, i: (b, 0, i, 0))
    else:
        h_in = h
        h_spec = pl.BlockSpec((None, tc, d), lambda b, i: (b, i, 0))
    vec = pl.BlockSpec((None, 1, d), lambda b, i: (b, 0, 0))
    return pl.pallas_call(
        functools.partial(_inproj_kernel, colmajor=colmajor, cn=1152),
        grid=(bsz, t_len // tc),
        in_specs=[h_spec, vec, vec,
                  pl.BlockSpec((1, d), lambda b, i: (0, 0)),
                  pl.BlockSpec((d, n), lambda b, i: (0, 0), pipeline_mode=pl.Buffered(1))],
        out_specs=pl.BlockSpec((None, tc, n), lambda b, i: (b, i, 0)),
        out_shape=jax.ShapeDtypeStruct((bsz, t_len, n), F32),
        compiler_params=_params(),
        name="inproj",
    )(h_in, shift, scale, npre, w)


def _outproj_kernel(ya_ref, yb_ref, yc_ref, w_ref, h_ref, gt_ref, npost_ref, o_ref, *, colmajor):
    dm = ya_ref.shape[-1]
    d = npost_ref.shape[-1]
    o = (jnp.dot(ya_ref[...].astype(BF16), w_ref[0:dm, :], preferred_element_type=F32)
         + jnp.dot(yb_ref[...].astype(BF16), w_ref[dm:2 * dm, :], preferred_element_type=F32)
         + jnp.dot(yc_ref[...].astype(BF16), w_ref[2 * dm:3 * dm, :], preferred_element_type=F32))
    upd = gt_ref[...] * (_rms(o) * npost_ref[...])
    if colmajor:
        rows = h_ref.shape[0]
        for j in range(h_ref.shape[1]):
            o_ref[:, j, :] = h_ref[:, j, :] + upd[j * rows:(j + 1) * rows, :]
    else:
        o_ref[...] = h_ref[...] + upd


def _outproj(ya, yb, yc, w, h, gate, npost, *, colmajor):
    bsz, t_len, d = h.shape
    dm = ya.shape[-1]
    tc = min(PROJ_TILE, t_len)
    if colmajor:
        rows = t_len // GRID_W
        h_in = h.reshape(bsz, rows, GRID_W, d)
        h_spec = pl.BlockSpec((None, rows, tc // rows, d), lambda b, i: (b, 0, i, 0))
    else:
        h_in = h
        h_spec = pl.BlockSpec((None, tc, d), lambda b, i: (b, i, 0))
    y_spec = pl.BlockSpec((None, tc, dm), lambda b, i: (b, i, 0))
    out = pl.pallas_call(
        functools.partial(_outproj_kernel, colmajor=colmajor),
        grid=(bsz, t_len // tc),
        in_specs=[y_spec, y_spec, y_spec,
                  pl.BlockSpec((3 * dm, d), lambda b, i: (0, 0)),
                  h_spec,
                  pl.BlockSpec((None, 1, d), lambda b, i: (b, 0, 0)),
                  pl.BlockSpec((1, d), lambda b, i: (0, 0))],
        out_specs=h_spec,
        out_shape=jax.ShapeDtypeStruct(h_in.shape, F32),
        compiler_params=_params(),
        name="outproj",
    )(ya, yb, yc, w, h_in, gate, npost)
    return out.reshape(bsz, t_len, d)


def _conv_tile(pad_ref, cw_ref, i):
    base = i * ROW_TILE + SUBLANES - CONV_PAD
    acc = cw_ref[0:1, :] * pad_ref[base:base + ROW_TILE, :]
    for j in range(1, CONV_W):
        acc = acc + cw_ref[j:j + 1, :] * pad_ref[base + j:base + j + ROW_TILE, :]
    return acc


def _fill_pad(pad_ref, x_ref, t_len):
    zeros = jnp.zeros((SUBLANES, pad_ref.shape[-1]), F32)
    pad_ref[0:SUBLANES, :] = zeros
    pad_ref[SUBLANES:SUBLANES + t_len, :] = x_ref[...]
    pad_ref[SUBLANES + t_len:2 * SUBLANES + t_len, :] = zeros


def _rglru_kernel(xc_ref, gc_ref, xx_ref, gx_ref, cw_ref, cb_ref, wg_ref, br_ref, bi_ref, lam_ref,
                  oc_ref, ox_ref, pad_ref, h_s, xc_s):
    dh = xc_ref.shape[-1]
    half = dh // 2
    nv = RG_SCAN_STEPS
    sub_rows = nv * SUBLANES
    n_sub = ROW_TILE // sub_rows
    sp = _softplus(-lam_ref[...])
    sub = lax.broadcasted_iota(jnp.int32, (SUBLANES, dh), 0)

    nlb = dh // LANES

    def conv_tile(i):
        vr = []
        for b in range(n_sub):
            base = i * ROW_TILE + b * sub_rows + SUBLANES - CONV_PAD
            vr.append([jnp.concatenate([pad_ref[lb, pl.ds(base + j, SUBLANES, stride=nv), :]
                                        for lb in range(nlb)], axis=1) for j in range(nv + CONV_W - 1)])
        xc = cb_ref[...]
        for tap in range(CONV_W):
            xc = xc + cw_ref[tap:tap + 1, :] * jnp.concatenate(
                [vr[b][tap + j] for b in range(n_sub) for j in range(nv)], axis=0)
        return xc

    def gates(xc, dirn):
        pre = [_dot(xc[:, hf * half:(hf + 1) * half], wg_ref[hf, :, dirn * dh:(dirn + 1) * dh])
               for hf in range(2)]
        r_pre = jnp.concatenate([pre[0][:, :half], pre[1][:, :half]], axis=1)
        i_pre = jnp.concatenate([pre[0][:, half:], pre[1][:, half:]], axis=1)
        r = _sigmoid(r_pre + br_ref[dirn:dirn + 1, :])
        ig = _sigmoid(i_pre + bi_ref[dirn:dirn + 1, :])
        log_a = -RG_C * r * sp[dirn:dirn + 1, :]
        a = jnp.exp(log_a)
        y = -jnp.tanh(log_a) * (a * a + 1.0)
        u = jnp.where(y > 0.0, y * lax.rsqrt(y), 0.0) * (ig * xc)
        return a, u

    def tile_scan(a, u, carry, rev):
        out = [None] * n_sub
        for b in (range(n_sub)[::-1] if rev else range(n_sub)):
            rows = slice(b * sub_rows, (b + 1) * sub_rows)
            out[b], carry = sub_scan(a[rows, :], u[rows, :], carry, rev)
        return [h for hs in out for h in hs], carry

    def sub_scan(a, u, carry, rev):
        order = list(range(nv))[::-1] if rev else list(range(nv))
        a_cum, h_loc = [None] * nv, [None] * nv
        prev = None
        for j in order:
            aj, uj = a[j * SUBLANES:(j + 1) * SUBLANES, :], u[j * SUBLANES:(j + 1) * SUBLANES, :]
            if prev is None:
                a_cum[j], h_loc[j] = aj, uj
            else:
                a_cum[j], h_loc[j] = aj * a_cum[prev], aj * h_loc[prev] + uj
            prev = j
        p, r = a_cum[prev], h_loc[prev]
        for sft in (1, 2, 4):
            keep = (sub < SUBLANES - sft) if rev else (sub >= sft)
            amt = SUBLANES - sft if rev else sft
            r = r + p * jnp.where(keep, pltpu.roll(r, amt, 0), 0.0)
            p = p * jnp.where(keep, pltpu.roll(p, amt, 0), 1.0)
        state = p * carry + r
        keep = (sub < SUBLANES - 1) if rev else (sub >= 1)
        c_in = jnp.where(keep, pltpu.roll(state, SUBLANES - 1 if rev else 1, 0), carry)
        last = 0 if rev else SUBLANES - 1
        return [h_loc[j] + a_cum[j] * c_in for j in range(nv)], state[last:last + 1, :]

    def seg(x_ref, g_ref, o_ref, t_len, h0):
        n_tiles = t_len // ROW_TILE
        zeros = jnp.zeros((SUBLANES, LANES), F32)
        for lb in range(nlb):
            pad_ref[lb, 0:SUBLANES, :] = zeros
            pad_ref[lb, SUBLANES:SUBLANES + t_len, :] = x_ref[:, lb * LANES:(lb + 1) * LANES]
            pad_ref[lb, SUBLANES + t_len:2 * SUBLANES + t_len, :] = zeros
        carries = []
        for dirn in range(2):
            def tile_body(it, carry, dirn=dirn):
                i = it if dirn == 0 else n_tiles - 1 - it
                tile_rows = pl.ds(pl.multiple_of(i * ROW_TILE, ROW_TILE), ROW_TILE)
                if dirn == 0:
                    xc = conv_tile(i)
                    xc_s[tile_rows, :] = xc
                else:
                    xc = xc_s[tile_rows, :]
                a, u = gates(xc, dirn)
                hs, carry = tile_scan(a, u, carry, dirn == 1)
                for bj in range(n_sub * nv):
                    start = i * ROW_TILE + (bj // nv) * sub_rows + bj % nv
                    for lb in range(nlb):
                        h_s[dirn, lb, pl.ds(start, SUBLANES, stride=nv), :] = (
                            hs[bj][:, lb * LANES:(lb + 1) * LANES])
                return carry

            carries.append(lax.fori_loop(0, n_tiles, tile_body, h0[dirn]))
        for i in range(n_tiles):
            rows = slice(i * ROW_TILE, (i + 1) * ROW_TILE)
            hsum = jnp.concatenate([h_s[0, lb, rows, :] + h_s[1, lb, rows, :] for lb in range(nlb)], axis=1)
            o_ref[rows, :] = hsum * _silu(g_ref[rows, :])
        return tuple(carries)

    zero = jnp.zeros((1, dh), F32)
    states = seg(xc_ref, gc_ref, oc_ref, xc_ref.shape[0], (zero, zero))
    seg(xx_ref, gx_ref, ox_ref, xx_ref.shape[0], states)


def _rglru(zc, zx, cw, cb, wg, br, bi, lam):
    bsz, tc_len, _ = zc.shape
    tx_len = zx.shape[1]
    dh = D_HEADS
    nb = dh // LANES

    def zspec(t_len, col):
        return pl.BlockSpec((None, t_len, dh), lambda b: (b, 0, col // nb))

    def full(a):
        return pl.BlockSpec(a.shape, lambda b: (0,) * a.ndim)

    return pl.pallas_call(
        _rglru_kernel,
        grid=(bsz,),
        in_specs=[zspec(tc_len, COL_XA), zspec(tc_len, COL_GA), zspec(tx_len, COL_XA), zspec(tx_len, COL_GA),
                  full(cw), full(cb), full(wg), full(br), full(bi), full(lam)],
        out_specs=[pl.BlockSpec((None, tc_len, dh), lambda b: (b, 0, 0)),
                   pl.BlockSpec((None, tx_len, dh), lambda b: (b, 0, 0))],
        out_shape=[jax.ShapeDtypeStruct((bsz, tc_len, dh), F32),
                   jax.ShapeDtypeStruct((bsz, tx_len, dh), F32)],
        scratch_shapes=[pltpu.VMEM((nb, tx_len + 2 * SUBLANES, LANES), F32),
                        pltpu.VMEM((2, nb, tx_len, LANES), F32),
                        pltpu.VMEM((tx_len, dh), F32)],
        compiler_params=_params(),
        name="rglru",
    )(zc, zc, zx, zx, cw, cb, wg, br, bi, lam)


GDN_UNROLL = 8
GDN_SPLICES = 4


def _gdn_masks():
    c = CHUNK_B
    i = np.arange(c)[:, None]
    k = np.arange(c)[None, :]
    stacked = np.stack([np.concatenate([k <= i, k > i], 0), np.concatenate([k >= i, k < i], 0)])
    return np.tile(stacked, (1, 1, 3)).astype(np.float32)


def _split3_rows(x):
    hi = x.astype(BF16)
    r1 = x - hi.astype(F32)
    mid = r1.astype(BF16)
    lo = (r1 - mid.astype(F32)).astype(BF16)
    return jnp.concatenate([hi, mid, lo], axis=0)


INV_BLOCK = 16


def _gdn_kernel(qc_ref, kc_ref, vc_ref, abc_ref, gtc_ref, qx_ref, kx_ref, vx_ref, abx_ref, gtx_ref,
                cwq_ref, cwk_ref, cwv_ref, prm_ref, nw_ref, mst_ref, oc_ref, ox_ref,
                pad_ref, q_s, k_s, v_s, gf_s, bf_s, gb_s, bb_s, o_s, qp_s, p_s, n_s, et_s,
                o_c, qp_c, p_c, n_c, et_c):
    head = pl.program_id(1)
    c = CHUNK_B
    ii = lax.broadcasted_iota(jnp.int32, (c, c), 0)
    jj = lax.broadcasted_iota(jnp.int32, (c, c), 1)
    eye = (ii == jj).astype(F32)
    a_mask = (ii > jj, ii < jj)
    s_mask = (ii >= jj, ii <= jj)
    d_mask = (ii // INV_BLOCK) == (jj // INV_BLOCK)
    f_mask = []
    size = INV_BLOCK
    while size < c:
        same = (ii // (2 * size)) == (jj // (2 * size))
        hi_i, hi_j = (ii // size) % 2, (jj // size) % 2
        f_mask.append((same & (hi_i == 1) & (hi_j == 0), same & (hi_i == 0) & (hi_j == 1)))
        size *= 2
    sel_r = lax.broadcasted_iota(jnp.int32, (3 * LANES, 4 * LANES), 0) % LANES
    sel_b = lax.broadcasted_iota(jnp.int32, (3 * LANES, 4 * LANES), 1) // LANES
    sel4 = (sel_r == head + H_B * (2 * (sel_b % 2) + sel_b // 2)).astype(BF16)
    lane = lax.broadcasted_iota(jnp.int32, (ROW_TILE, LANES), 1)

    def prep(q_ref, k_ref, v_ref, ab_ref, t_len):
        n_tiles = t_len // ROW_TILE
        for src, dst, cw_ref, kind in ((q_ref, q_s, cwq_ref, "q"), (k_ref, k_s, cwk_ref, "k"),
                                       (v_ref, v_s, cwv_ref, "v")):
            _fill_pad(pad_ref, src, t_len)
            for i in range(n_tiles):
                xc = _silu(_conv_tile(pad_ref, cw_ref, i))
                if kind != "v":
                    xc = xc * lax.rsqrt(jnp.sum(xc * xc, axis=-1, keepdims=True) + EPS)
                if kind == "q":
                    xc = xc * (DK_B ** -0.5)
                dst[i * ROW_TILE:(i + 1) * ROW_TILE, :] = xc
        for i in range(n_tiles):
            rows = slice(i * ROW_TILE, (i + 1) * ROW_TILE)
            ab = ab_ref[rows, :]
            gval = -jnp.exp(prm_ref[0:1, :]) * _softplus(ab + prm_ref[1:2, :])
            gbv = jnp.where(lane < 2 * H_B, gval, _sigmoid(ab))
            hi = gbv.astype(BF16)
            r1 = gbv - hi.astype(F32)
            mid = r1.astype(BF16)
            lo = (r1 - mid.astype(F32)).astype(BF16)
            bc = jnp.dot(jnp.concatenate([hi, mid, lo], axis=1), sel4, preferred_element_type=F32)
            for j, dst in enumerate((gf_s, bf_s, gb_s, bb_s)):
                dst[rows, :] = bc[:, j * LANES:(j + 1) * LANES]

    def chunk_terms(items, scr, hooks=()):
        o_s, qp_s, p_s, n_s, et_s = scr
        hooks = list(hooks)
        per_splice = -(-len(hooks) // GDN_SPLICES)

        def splice():
            for _ in range(min(per_splice, len(hooks))):
                hooks.pop(0)()

        m = range(len(items))
        dirs = [d for d, _ in items]
        rows = [pl.ds(pl.multiple_of(ci * c, c), c) for _, ci in items]
        prow = [pl.ds(pl.multiple_of(ci * LANES, LANES), LANES) for _, ci in items]
        erow = [pl.ds(pl.multiple_of(ci * SUBLANES, SUBLANES), SUBLANES) for _, ci in items]
        q = [q_s[r, :] for r in rows]
        k = [k_s[r, :] for r in rows]
        v = [v_s[r, :] for r in rows]
        g = [(gf_s, gb_s)[d][r, :] for d, r in zip(dirs, rows)]
        beta = [(bf_s, bb_s)[d][r, :] for d, r in zip(dirs, rows)]
        cum = [jnp.dot(mst_ref[dirs[i]], _split3_rows(g[i]), preferred_element_type=F32) for i in m]
        kb = [k[i] * beta[i] for i in m]
        kq = [_dot_nt(jnp.concatenate([kb[i], q[i]], axis=0), k[i]) for i in m]
        splice()
        e_in = [jnp.exp(cum[i][0:c]) for i in m]
        e_rest = [jnp.exp(cum[i][c:2 * c]) for i in m]
        cs = [cum[i][0:c, 0:c] for i in m]
        dec = [jnp.exp(jnp.where(s_mask[dirs[i]], cs[i] - cs[i].T, 0.0)) for i in m]
        for i in m:
            et_s[dirs[i], erow[i], :] = jnp.exp((cum[i][0:SUBLANES] + cum[i][c:c + SUBLANES]))
        a = [jnp.where(a_mask[dirs[i]], kq[i][:c] * dec[i], 0.0) for i in m]
        qk = [jnp.where(s_mask[dirs[i]], kq[i][c:] * dec[i], 0.0) for i in m]
        dg = [jnp.where(d_mask, a[i], 0.0) for i in m]
        x1 = [_dot(dg[i], dg[i]) for i in m]
        x2 = [_dot(x1[i], x1[i]) for i in m]
        splice()
        st = [_dot(jnp.concatenate([x1[i], x2[i]], axis=0), x2[i]) for i in m]
        q2 = [x1[i] + x2[i] + st[i][:c] for i in m]
        q3 = [q2[i] + st[i][c:] + _dot(q2[i], st[i][c:]) for i in m]
        imd = [eye - dg[i] for i in m]
        t_inv = [imd[i] + _dot(imd[i], q3[i]) for i in m]
        splice()
        for fm in f_mask:
            tf = [_dot(t_inv[i], jnp.where(fm[dirs[i]], a[i], 0.0)) for i in m]
            t_inv = [t_inv[i] - _dot(tf[i], t_inv[i]) for i in m]
        splice()
        uw = [_dot(t_inv[i], jnp.concatenate([v[i] * beta[i], kb[i] * e_in[i]], axis=1)) for i in m]
        res = [_dot(jnp.concatenate([qk[i], (k[i] * e_rest[i]).T], axis=0), uw[i]) for i in m]
        for i in m:
            n_s[dirs[i], prow[i], :] = res[i][c:, :LANES]
            p_s[dirs[i], prow[i], :] = res[i][c:, LANES:].astype(BF16)
            o_s[dirs[i], rows[i], :] = res[i][:c, :LANES]
            qp_s[dirs[i], rows[i], :] = (q[i] * e_in[i] - res[i][:c, LANES:]).astype(BF16)

    state = [(jnp.zeros((LANES, LANES), F32),) * 2]

    def seg(q_ref, k_ref, v_ref, ab_ref, gt_ref, o_ref, scr, pending):
        o_s, qp_s, p_s, n_s, et_s = scr
        t_len = q_ref.shape[0]
        n_chunks = t_len // c
        prep(q_ref, k_ref, v_ref, ab_ref, t_len)

        unroll = min(GDN_UNROLL, n_chunks)
        n_full = n_chunks // unroll

        def terms(start, size, hooks):
            chunk_terms([(dirn, start + j if dirn == 0 else n_chunks - 1 - (start + j))
                         for j in range(size) for dirn in range(2)], scr, hooks)

        def state_hooks(start, size):
            def hook(n):
                def run():
                    st, new = state[0], []
                    for dirn, ci in ((0, n), (1, n_chunks - 1 - n)):
                        rows = pl.ds(pl.multiple_of(ci * c, c), c)
                        prow = pl.ds(pl.multiple_of(ci * LANES, LANES), LANES)
                        sb = st[dirn].astype(BF16)
                        o_s[dirn, rows, :] = o_s[dirn, rows, :] + jnp.dot(qp_s[dirn, rows, :], sb,
                                                                         preferred_element_type=F32)
                        e_tot = et_s[dirn, pl.ds(pl.multiple_of(ci * SUBLANES, SUBLANES), 1), :]
                        new.append(st[dirn] * e_tot + n_s[dirn, prow, :]
                                   - jnp.dot(p_s[dirn, prow, :], sb, preferred_element_type=F32))
                    state[0] = tuple(new)
                return run
            return [hook(start + j) for j in range(size)]

        def finish():
            for i in range(t_len // ROW_TILE):
                rows = slice(i * ROW_TILE, (i + 1) * ROW_TILE)
                o = o_s[0, rows, :] + o_s[1, rows, :]
                o_ref[rows, :] = _rms(o) * nw_ref[...] * _silu(gt_ref[rows, :])

        terms(0, unroll, pending)

        def pass_body(p, st):
            state[0] = st
            terms(p * unroll, unroll, state_hooks((p - 1) * unroll, unroll))
            return state[0]

        state[0] = lax.fori_loop(1, n_full, pass_body, state[0])
        return state_hooks((n_full - 1) * unroll, unroll) + [finish]

    pending = seg(qc_ref, kc_ref, vc_ref, abc_ref, gtc_ref, oc_ref, (o_c, qp_c, p_c, n_c, et_c), [])
    for run in seg(qx_ref, kx_ref, vx_ref, abx_ref, gtx_ref, ox_ref, (o_s, qp_s, p_s, n_s, et_s), pending):
        run()


def _gdn(zc, zx, cwb, prm, nw):
    bsz, tc_len, _ = zc.shape
    tx_len = zx.shape[1]

    def zspec(t_len, col, per_head=True):
        if per_head:
            return pl.BlockSpec((None, t_len, LANES), lambda b, h: (b, 0, col + h))
        return pl.BlockSpec((None, t_len, LANES), lambda b, h: (b, 0, col))

    def seg_specs(t_len):
        return [zspec(t_len, COL_Q), zspec(t_len, COL_K), zspec(t_len, COL_V),
                zspec(t_len, COL_AB, False), zspec(t_len, COL_GB)]

    def cw_spec(off):
        return pl.BlockSpec((CONV_W, LANES), lambda b, h: (0, off + h))

    mst = jnp.asarray(_gdn_masks(), BF16)

    def terms_scratch(t_len):
        n_chunks = t_len // CHUNK_B
        return [pltpu.VMEM((2, t_len, LANES), F32),
                pltpu.VMEM((2, t_len, LANES), BF16),
                pltpu.VMEM((2, n_chunks * LANES, LANES), BF16),
                pltpu.VMEM((2, n_chunks * LANES, LANES), F32),
                pltpu.VMEM((2, n_chunks * SUBLANES, LANES), F32)]

    return pl.pallas_call(
        _gdn_kernel,
        grid=(bsz, H_B),
        in_specs=seg_specs(tc_len) + seg_specs(tx_len) + [
            cw_spec(0), cw_spec(H_B), cw_spec(2 * H_B),
            pl.BlockSpec(prm.shape, lambda b, h: (0, 0)),
            pl.BlockSpec(nw.shape, lambda b, h: (0, 0)),
            pl.BlockSpec(mst.shape, lambda b, h: (0, 0, 0))],
        out_specs=[pl.BlockSpec((None, tc_len, LANES), lambda b, h: (b, 0, h)),
                   pl.BlockSpec((None, tx_len, LANES), lambda b, h: (b, 0, h))],
        out_shape=[jax.ShapeDtypeStruct((bsz, tc_len, D_HEADS), F32),
                   jax.ShapeDtypeStruct((bsz, tx_len, D_HEADS), F32)],
        scratch_shapes=[pltpu.VMEM((tx_len + 2 * SUBLANES, LANES), F32)]
                       + [pltpu.VMEM((tx_len, LANES), F32) for _ in range(7)]
                       + terms_scratch(tx_len) + terms_scratch(tc_len),
        compiler_params=_params(),
        name="gdn",
    )(zc, zc, zc, zc, zc, zx, zx, zx, zx, zx, cwb, cwb, cwb, prm, nw, mst)


_LEVELS = (1, 2, 4, 8, 16, 32, 64)


def _gla_masks():
    n = TILE_C
    i = np.arange(n)[:, None]
    k = np.arange(n)[None, :]
    sums, scores = [], []
    for lv in _LEVELS:
        same = (i // (2 * lv)) == (k // (2 * lv))
        hi_i, hi_k = (i // lv) % 2, (k // lv) % 2
        if lv < SUBLANES:
            sums.append(same & np.where(hi_i == 1, (hi_k == 1) & (k <= i), (hi_k == 0) & (k > i)))
        scores.append(same & (hi_i == 1) & (hi_k == 0))
    sums += [k <= i, k > i]
    scores.append(i == k)
    sums_f = np.stack(sums).astype(np.float32)
    scores_f = np.stack(scores).astype(np.float32)
    sums_all = np.stack([sums_f, sums_f[:, ::-1, ::-1]]).reshape(2, len(sums) * n, n)
    scores_all = np.stack([scores_f, scores_f[:, ::-1, ::-1]])
    return np.tile(sums_all, (1, 1, 2)), scores_all


GLA_UNROLL = 4


def _gla_kernel(qc_ref, ffc_ref, fbc_ref, ic_ref, gtc_ref, qx_ref, ffx_ref, fbx_ref, ix_ref, gtx_ref,
                lb_ref, nw_ref, msum_ref, mscore_ref, oc_ref, ox_ref, oi_s, qd_s, u_s, et_s, *, layer):
    n = TILE_C
    n_lv = len(_LEVELS)
    depth = lb_ref.shape[0] // 2
    lbs = []
    for dirn in range(2):
        rows = [lb_ref[2 * j + dirn:2 * j + dirn + 1, :] for j in range(depth)]
        mx = functools.reduce(jnp.maximum, rows)
        ex = [jnp.exp(r - mx) for r in rows]
        den = functools.reduce(lambda p, q: p + q, ex)
        acc = jnp.zeros_like(mx)
        for j in range(1, layer + 1):
            acc = acc + ex[j] / den
        lbs.append(acc)

    def tile_terms(items, q_ref, f_refs, i_ref):
        m = range(len(items))
        dirs = [d for d, _ in items]
        rows = [pl.ds(pl.multiple_of(ti * n, n), n) for _, ti in items]
        q = [_silu(q_ref[r, :]) for r in rows]
        v = [i_ref[r, :] for r in rows]
        fg = [lbs[dirs[i]] + (1.0 - lbs[dirs[i]]) * _sigmoid(f_refs[dirs[i]][rows[i], :]) for i in m]
        k = [1.0 - fg[i] for i in m]
        lf = [jnp.log(fg[i]) for i in m]
        hi = [lf[i].astype(BF16) for i in m]
        lf2 = [jnp.concatenate([hi[i], (lf[i] - hi[i].astype(F32)).astype(BF16)], axis=0) for i in m]

        sums = [jnp.dot(msum_ref[dirs[i]], lf2[i], preferred_element_type=F32) for i in m]

        def csum(j):
            return [sums[i][j * n:(j + 1) * n, :] for i in m]

        n_small = sum(1 for lv in _LEVELS if lv < SUBLANES)
        c_in, c_rest = csum(n_small), csum(n_small + 1)

        def boundary_exponent(i, lv):
            off = lv - 1 if dirs[i] == 0 else lv
            ref = jnp.concatenate(
                [jnp.broadcast_to(c_in[i][blk * 2 * lv + off:blk * 2 * lv + off + 1, :], (2 * lv, LANES))
                 for blk in range(n // (2 * lv))], axis=0)
            return -jnp.abs(c_in[i] - ref)

        qb = [q[i].astype(BF16) for i in m]
        kb = [k[i].astype(BF16) for i in m]
        scores = [mscore_ref[dirs[i], n_lv] * _dot_nt(qb[i], kb[i]).astype(BF16) for i in m]
        for j, lv in enumerate(_LEVELS):
            xs = csum(j) if lv < SUBLANES else [boundary_exponent(i, lv) for i in m]
            e = [jnp.exp(x).astype(BF16) for x in xs]
            p = [_dot_nt(qb[i] * e[i], kb[i] * e[i]) for i in m]
            scores = [scores[i] + mscore_ref[dirs[i], j] * p[i].astype(BF16) for i in m]
        o_intra = [_dot(scores[i], v[i]) for i in m]
        u = [_dot_tn(v[i], k[i] * jnp.exp(c_rest[i])) for i in m]
        for i, (_, ti) in enumerate(items):
            oi_s[dirs[i], rows[i], :] = o_intra[i]
            qd_s[dirs[i], rows[i], :] = (q[i] * jnp.exp(c_in[i])).astype(BF16)
            u_s[dirs[i], rows[i], :] = u[i]
            et_s[dirs[i], pl.ds(pl.multiple_of(ti * SUBLANES, SUBLANES), SUBLANES), :] = jnp.exp(
                (c_in[i] + c_rest[i])[0:SUBLANES, :])

    def seg(q_ref, ff_ref, fb_ref, i_ref, gt_ref, o_ref, states):
        t_len = q_ref.shape[0]
        n_tiles = t_len // n

        unroll = min(GLA_UNROLL, n_tiles)
        n_pass = n_tiles // unroll

        def terms(p):
            tile_terms([(dirn, p * unroll + j if dirn == 0 else n_tiles - 1 - (p * unroll + j))
                        for j in range(unroll) for dirn in range(2)], q_ref, (ff_ref, fb_ref), i_ref)

        def state_steps(p, st):
            for j in range(unroll):
                ti = p * unroll + j
                new = []
                for dirn, tj in ((0, ti), (1, n_tiles - 1 - ti)):
                    rows = pl.ds(pl.multiple_of(tj * n, n), n)
                    oi_s[dirn, rows, :] = oi_s[dirn, rows, :] + lax.dot_general(
                        qd_s[dirn, rows, :], st[dirn].astype(BF16), (((1,), (1,)), ((), ())),
                        preferred_element_type=F32)
                    e_tot = et_s[dirn, pl.ds(pl.multiple_of(tj * SUBLANES, SUBLANES), 1), :]
                    new.append(st[dirn] * e_tot + u_s[dirn, rows, :])
                st = tuple(new)
            return st

        terms(0)

        def pass_body(p, st):
            st = state_steps(p - 1, st)
            terms(p)
            return st

        states = lax.fori_loop(1, n_pass, pass_body, states)
        states = state_steps(n_pass - 1, states)
        for i in range(t_len // ROW_TILE):
            rows = slice(i * ROW_TILE, (i + 1) * ROW_TILE)
            o = oi_s[0, rows, :] + oi_s[1, rows, :]
            o_ref[rows, :] = _rms(o) * nw_ref[...] * _silu(gt_ref[rows, :])
        return states

    zero = jnp.zeros((LANES, LANES), F32)
    states = seg(qc_ref, ffc_ref, fbc_ref, ic_ref, gtc_ref, oc_ref, (zero, zero))
    seg(qx_ref, ffx_ref, fbx_ref, ix_ref, gtx_ref, ox_ref, states)


def _gla(zc, zx, lb, nw, msum, mscore, layer):
    bsz, tc_len, _ = zc.shape
    tx_len = zx.shape[1]

    def zspec(t_len, col):
        return pl.BlockSpec((None, t_len, LANES), lambda b, h: (b, 0, col + h))

    def seg_specs(t_len):
        return [zspec(t_len, COL_QC), zspec(t_len, COL_FF), zspec(t_len, COL_FB),
                zspec(t_len, COL_IC), zspec(t_len, COL_GC)]

    return pl.pallas_call(
        functools.partial(_gla_kernel, layer=layer),
        grid=(bsz, H_C),
        in_specs=seg_specs(tc_len) + seg_specs(tx_len) + [
            pl.BlockSpec((lb.shape[0], LANES), lambda b, h: (0, h)),
            pl.BlockSpec(nw.shape, lambda b, h: (0, 0)),
            pl.BlockSpec(msum.shape, lambda b, h: (0, 0, 0)),
            pl.BlockSpec(mscore.shape, lambda b, h: (0, 0, 0, 0))],
        out_specs=[pl.BlockSpec((None, tc_len, LANES), lambda b, h: (b, 0, h)),
                   pl.BlockSpec((None, tx_len, LANES), lambda b, h: (b, 0, h))],
        out_shape=[jax.ShapeDtypeStruct((bsz, tc_len, D_HEADS), F32),
                   jax.ShapeDtypeStruct((bsz, tx_len, D_HEADS), F32)],
        scratch_shapes=[pltpu.VMEM((2, tx_len, LANES), F32),
                        pltpu.VMEM((2, tx_len, LANES), BF16),
                        pltpu.VMEM((2, tx_len, LANES), F32),
                        pltpu.VMEM((2, (tx_len // TILE_C) * SUBLANES, LANES), F32)],
        compiler_params=_params(),
        name="hgrn2",
    )(zc, zc, zc, zc, zc, zx, zx, zx, zx, zx, lb, nw, msum, mscore)


def _block_diag_halves(w):
    w4 = w.reshape(2, H_A // 2, HD_A, HD_A)
    eye = jnp.eye(H_A // 2, dtype=w.dtype)
    return jnp.einsum("ghij,hk->ghikj", w4, eye).reshape(2, (H_A // 2) * HD_A, (H_A // 2) * HD_A)


def kernel(x, c, ctx, c_ctx, w_ada, b_ada, norm_pre, norm_post, w_in, conv_a_w, conv_a_b, rg_w_r, rg_b_r,
           rg_w_i, rg_b_i, rg_lam, conv_b_w, gdn_a_log, gdn_dt_bias, gdn_norm, hg_lb, hg_norm, w_out):
    bsz, t_len, d = x.shape
    depth = w_ada.shape[0]
    dh = D_HEADS

    cs = jnp.concatenate([c, c_ctx[None, :], jnp.zeros((16 - bsz - 1, d), F32)], axis=0)
    mod = _ada(cs, w_ada, b_ada)

    qkv_end = 2 * dh + 3 * dh
    w_in_b = w_in.astype(BF16)
    w_in_p = jnp.concatenate(
        [w_in_b[:, :, :qkv_end], w_in_b[:, :, qkv_end + 4 * H_B:], w_in_b[:, :, qkv_end:qkv_end + 4 * H_B],
         jnp.zeros((depth, d, LANES - 4 * H_B), BF16)], axis=-1)
    w_out_b = w_out.astype(BF16)
    msum_np, mscore_np = _gla_masks()
    msum = jnp.asarray(msum_np, BF16)
    mscore = jnp.asarray(mscore_np, BF16)
    lb2 = hg_lb.reshape(depth * 2, dh)

    h, hc = x, ctx
    for l in range(depth):
        colmajor = l % 2 == 1
        mx = mod[l, :bsz].reshape(bsz, 1, 3 * d)
        mc = jnp.broadcast_to(mod[l, bsz].reshape(1, 1, 3 * d), (bsz, 1, 3 * d))
        npre = norm_pre[l].reshape(1, d)
        npost = norm_post[l].reshape(1, d)
        zc = _inproj(hc, mc[..., :d], mc[..., d:2 * d], npre, w_in_p[l], colmajor=False)
        zx = _inproj(h, mx[..., :d], mx[..., d:2 * d], npre, w_in_p[l], colmajor=colmajor)

        wg = jnp.concatenate([_block_diag_halves(rg_w_r[l, 0]), _block_diag_halves(rg_w_i[l, 0]),
                              _block_diag_halves(rg_w_r[l, 1]), _block_diag_halves(rg_w_i[l, 1])],
                             axis=-1).astype(BF16)
        ya_c, ya_x = _rglru(zc, zx, conv_a_w[l], conv_a_b[l].reshape(1, dh), wg,
                            rg_b_r[l], rg_b_i[l], rg_lam[l])

        prm = jnp.zeros((SUBLANES, LANES), F32)
        prm = prm.at[0, :2 * H_B].set(gdn_a_log[l].reshape(-1)).at[1, :2 * H_B].set(gdn_dt_bias[l].reshape(-1))
        yb_c, yb_x = _gdn(zc, zx, conv_b_w[l], prm, gdn_norm[l].reshape(1, LANES))

        yc_c, yc_x = _gla(zc, zx, lb2, hg_norm[l].reshape(1, LANES), msum, mscore, l)

        h = _outproj(ya_x, yb_x, yc_x, w_out_b[l], h, mx[..., 2 * d:], npost, colmajor=colmajor)
        if l < depth - 1:
            hc = _outproj(ya_c, yb_c, yc_c, w_out_b[l], hc, mc[..., 2 * d:], npost, colmajor=False)
    return h
```

```python
import functools

import numpy as np
import jax
import jax.numpy as jnp
from jax import lax
from jax.experimental import pallas as pl
from jax.experimental.pallas import tpu as pltpu

F32 = jnp.float32
BF16 = jnp.bfloat16
HI = lax.Precision.HIGHEST

GRID_W = 64
H_A, HD_A = 8, 64
H_B, DK_B = 4, 128
H_C = 4
D_HEADS = 512
CONV_W = 4
CONV_PAD = 2
RG_C = 8.0
EPS = 1e-6

LANES = 128
SUBLANES = 8
ROW_TILE = 256
PROJ_TILE = 512
RG_SCAN_STEPS = 8
CHUNK_B = 64
TILE_C = 128
VMEM_LIMIT = 56 * 1024 * 1024

COL_XA, COL_GA, COL_Q, COL_K, COL_V, COL_GB = 0, 4, 8, 12, 16, 20
COL_QC, COL_FF, COL_FB, COL_IC, COL_GC, COL_AB = 24, 28, 32, 36, 40, 44


def _dot(a, b):
    return jnp.dot(a.astype(BF16), b.astype(BF16), preferred_element_type=F32)


def _dot_nt(a, b):
    return lax.dot_general(a.astype(BF16), b.astype(BF16), (((1,), (1,)), ((), ())),
                           preferred_element_type=F32)


def _dot_tn(a, b):
    return lax.dot_general(a.astype(BF16), b.astype(BF16), (((0,), (0,)), ((), ())),
                           preferred_element_type=F32)


def _dot_hi(a, b):
    return jnp.dot(a, b, precision=HI, preferred_element_type=F32)


def _sigmoid(x):
    return 1.0 / (1.0 + jnp.exp(-x))


def _silu(x):
    return x * _sigmoid(x)


def _softplus(x):
    return jnp.maximum(x, 0.0) + jnp.log1p(jnp.exp(-jnp.abs(x)))


def _rms(x):
    return x * lax.rsqrt(jnp.mean(x * x, axis=-1, keepdims=True) + EPS)


def _params(**kw):
    return pltpu.CompilerParams(vmem_limit_bytes=VMEM_LIMIT, **kw)


def _ada_kernel(s_ref, w_ref, b_ref, o_ref):
    o_ref[...] = _dot_hi(_silu(s_ref[...]), w_ref[...]) + b_ref[...]


def _ada(cs, w_ada, b_ada):
    depth, d, n3 = w_ada.shape
    tn = 1024
    return pl.pallas_call(
        _ada_kernel,
        grid=(depth, n3 // tn),
        in_specs=[pl.BlockSpec((16, d), lambda l, j: (0, 0)),
                  pl.BlockSpec((None, d, tn), lambda l, j: (l, 0, j)),
                  pl.BlockSpec((None, 1, tn), lambda l, j: (l, 0, j))],
        out_specs=pl.BlockSpec((None, 16, tn), lambda l, j: (l, 0, j)),
        out_shape=jax.ShapeDtypeStruct((depth, 16, n3), F32),
        compiler_params=_params(),
        name="ada",
    )(cs, w_ada, b_ada.reshape(depth, 1, n3))


def _inproj_kernel(h_ref, sh_ref, sc_ref, npre_ref, w_ref, z_ref, *, colmajor, cn):
    if colmajor:
        x = jnp.concatenate([h_ref[:, j, :] for j in range(h_ref.shape[1])], axis=0)
    else:
        x = h_ref[...]
    u = (_rms(x) * npre_ref[...] * (1.0 + sc_ref[...]) + sh_ref[...]).astype(BF16)
    for j in range(z_ref.shape[-1] // cn):
        z_ref[:, j * cn:(j + 1) * cn] = jnp.dot(u, w_ref[:, j * cn:(j + 1) * cn],
                                               preferred_element_type=F32)


def _inproj(h, shift, scale, npre, w, *, colmajor):
    bsz, t_len, d = h.shape
    n = w.shape[-1]
    tc = min(PROJ_TILE, t_len)
    if colmajor:
        rows = t_len // GRID_W
        h_in = h.reshape(bsz, rows, GRID_W, d)
        h_spec = pl.BlockSpec((None, rows, tc // rows, d), lambda b, i: (b, 0, i, 0))
    else:
        h_in = h
        h_spec = pl.BlockSpec((None, tc, d), lambda b, i: (b, i, 0))
    vec = pl.BlockSpec((None, 1, d), lambda b, i: (b, 0, 0))
    return pl.pallas_call(
        functools.partial(_inproj_kernel, colmajor=colmajor, cn=1152),
        grid=(bsz, t_len // tc),
        in_specs=[h_spec, vec, vec,
                  pl.BlockSpec((1, d), lambda b, i: (0, 0)),
                  pl.BlockSpec((d, n), lambda b, i: (0, 0), pipeline_mode=pl.Buffered(1))],
        out_specs=pl.BlockSpec((None, tc, n), lambda b, i: (b, i, 0)),
        out_shape=jax.ShapeDtypeStruct((bsz, t_len, n), F32),
        compiler_params=_params(),
        name="inproj",
    )(h_in, shift, scale, npre, w)


def _outproj_kernel(ya_ref, yb_ref, yc_ref, w_ref, h_ref, gt_ref, npost_ref, o_ref, *, colmajor):
    dm = ya_ref.shape[-1]
    d = npost_ref.shape[-1]
    o = (jnp.dot(ya_ref[...].astype(BF16), w_ref[0:dm, :], preferred_element_type=F32)
         + jnp.dot(yb_ref[...].astype(BF16), w_ref[dm:2 * dm, :], preferred_element_type=F32)
         + jnp.dot(yc_ref[...].astype(BF16), w_ref[2 * dm:3 * dm, :], preferred_element_type=F32))
    upd = gt_ref[...] * (_rms(o) * npost_ref[...])
    if colmajor:
        rows = h_ref.shape[0]
        for j in range(h_ref.shape[1]):
            o_ref[:, j, :] = h_ref[:, j, :] + upd[j * rows:(j + 1) * rows, :]
    else:
        o_ref[...] = h_ref[...] + upd


def _outproj(ya, yb, yc, w, h, gate, npost, *, colmajor):
    bsz, t_len, d = h.shape
    dm = ya.shape[-1]
    tc = min(PROJ_TILE, t_len)
    if colmajor:
        rows = t_len // GRID_W
        h_in = h.reshape(bsz, rows, GRID_W, d)
        h_spec = pl.BlockSpec((None, rows, tc // rows, d), lambda b, i: (b, 0, i, 0))
    else:
        h_in = h
        h_spec = pl.BlockSpec((None, tc, d), lambda b, i: (b, i, 0))
    y_spec = pl.BlockSpec((None, tc, dm), lambda b, i: (b, i, 0))
    out = pl.pallas_call(
        functools.partial(_outproj_kernel, colmajor=colmajor),
        grid=(bsz, t_len // tc),
        in_specs=[y_spec, y_spec, y_spec,
                  pl.BlockSpec((3 * dm, d), lambda b, i: (0, 0)),
                  h_spec,
                  pl.BlockSpec((None, 1, d), lambda b, i: (b, 0, 0)),
                  pl.BlockSpec((1, d), lambda b, i: (0, 0))],
        out_specs=h_spec,
        out_shape=jax.ShapeDtypeStruct(h_in.shape, F32),
        compiler_params=_params(),
        name="outproj",
    )(ya, yb, yc, w, h_in, gate, npost)
    return out.reshape(bsz, t_len, d)


def _conv_tile(pad_ref, cw_ref, i):
    base = i * ROW_TILE + SUBLANES - CONV_PAD
    acc = cw_ref[0:1, :] * pad_ref[base:base + ROW_TILE, :]
    for j in range(1, CONV_W):
        acc = acc + cw_ref[j:j + 1, :] * pad_ref[base + j:base + j + ROW_TILE, :]
    return acc


def _fill_pad(pad_ref, x_ref, t_len):
    zeros = jnp.zeros((SUBLANES, pad_ref.shape[-1]), F32)
    pad_ref[0:SUBLANES, :] = zeros
    pad_ref[SUBLANES:SUBLANES + t_len, :] = x_ref[...]
    pad_ref[SUBLANES + t_len:2 * SUBLANES + t_len, :] = zeros


def _rglru_kernel(xc_ref, gc_ref, xx_ref, gx_ref, cw_ref, cb_ref, wg_ref, br_ref, bi_ref, lam_ref,
                  oc_ref, ox_ref, pad_ref, h_s, xc_s):
    dh = xc_ref.shape[-1]
    half = dh // 2
    nv = RG_SCAN_STEPS
    sub_rows = nv * SUBLANES
    n_sub = ROW_TILE // sub_rows
    sp = _softplus(-lam_ref[...])
    sub = lax.broadcasted_iota(jnp.int32, (SUBLANES, dh), 0)

    nlb = dh // LANES

    def conv_tile(i):
        vr = []
        for b in range(n_sub):
            base = i * ROW_TILE + b * sub_rows + SUBLANES - CONV_PAD
            vr.append([jnp.concatenate([pad_ref[lb, pl.ds(base + j, SUBLANES, stride=nv), :]
                                        for lb in range(nlb)], axis=1) for j in range(nv + CONV_W - 1)])
        xc = cb_ref[...]
        for tap in range(CONV_W):
            xc = xc + cw_ref[tap:tap + 1, :] * jnp.concatenate(
                [vr[b][tap + j] for b in range(n_sub) for j in range(nv)], axis=0)
        return xc

    def gates(xc, dirn):
        pre = [_dot(xc[:, hf * half:(hf + 1) * half], wg_ref[hf, :, dirn * dh:(dirn + 1) * dh])
               for hf in range(2)]
        r_pre = jnp.concatenate([pre[0][:, :half], pre[1][:, :half]], axis=1)
        i_pre = jnp.concatenate([pre[0][:, half:], pre[1][:, half:]], axis=1)
        r = _sigmoid(r_pre + br_ref[dirn:dirn + 1, :])
        ig = _sigmoid(i_pre + bi_ref[dirn:dirn + 1, :])
        log_a = -RG_C * r * sp[dirn:dirn + 1, :]
        a = jnp.exp(log_a)
        y = -jnp.tanh(log_a) * (a * a + 1.0)
        u = jnp.where(y > 0.0, y * lax.rsqrt(y), 0.0) * (ig * xc)
        return a, u

    def tile_scan(a, u, carry, rev):
        out = [None] * n_sub
        for b in (range(n_sub)[::-1] if rev else range(n_sub)):
            rows = slice(b * sub_rows, (b + 1) * sub_rows)
            out[b], carry = sub_scan(a[rows, :], u[rows, :], carry, rev)
        return [h for hs in out for h in hs], carry

    def sub_scan(a, u, carry, rev):
        order = list(range(nv))[::-1] if rev else list(range(nv))
        a_cum, h_loc = [None] * nv, [None] * nv
        prev = None
        for j in order:
            aj, uj = a[j * SUBLANES:(j + 1) * SUBLANES, :], u[j * SUBLANES:(j + 1) * SUBLANES, :]
            if prev is None:
                a_cum[j], h_loc[j] = aj, uj
            else:
                a_cum[j], h_loc[j] = aj * a_cum[prev], aj * h_loc[prev] + uj
            prev = j
        p, r = a_cum[prev], h_loc[prev]
        for sft in (1, 2, 4):
            keep = (sub < SUBLANES - sft) if rev else (sub >= sft)
            amt = SUBLANES - sft if rev else sft
            r = r + p * jnp.where(keep, pltpu.roll(r, amt, 0), 0.0)
            p = p * jnp.where(keep, pltpu.roll(p, amt, 0), 1.0)
        state = p * carry + r
        keep = (sub < SUBLANES - 1) if rev else (sub >= 1)
        c_in = jnp.where(keep, pltpu.roll(state, SUBLANES - 1 if rev else 1, 0), carry)
        last = 0 if rev else SUBLANES - 1
        return [h_loc[j] + a_cum[j] * c_in for j in range(nv)], state[last:last + 1, :]

    def seg(x_ref, g_ref, o_ref, t_len, h0):
        n_tiles = t_len // ROW_TILE
        zeros = jnp.zeros((SUBLANES, LANES), F32)
        for lb in range(nlb):
            pad_ref[lb, 0:SUBLANES, :] = zeros
            pad_ref[lb, SUBLANES:SUBLANES + t_len, :] = x_ref[:, lb * LANES:(lb + 1) * LANES]
            pad_ref[lb, SUBLANES + t_len:2 * SUBLANES + t_len, :] = zeros
        carries = []
        for dirn in range(2):
            def tile_body(it, carry, dirn=dirn):
                i = it if dirn == 0 else n_tiles - 1 - it
                tile_rows = pl.ds(pl.multiple_of(i * ROW_TILE, ROW_TILE), ROW_TILE)
                if dirn == 0:
                    xc = conv_tile(i)
                    xc_s[tile_rows, :] = xc
                else:
                    xc = xc_s[tile_rows, :]
                a, u = gates(xc, dirn)
                hs, carry = tile_scan(a, u, carry, dirn == 1)
                for bj in range(n_sub * nv):
                    start = i * ROW_TILE + (bj // nv) * sub_rows + bj % nv
                    for lb in range(nlb):
                        h_s[dirn, lb, pl.ds(start, SUBLANES, stride=nv), :] = (
                            hs[bj][:, lb * LANES:(lb + 1) * LANES])
                return carry

            carries.append(lax.fori_loop(0, n_tiles, tile_body, h0[dirn]))
        for i in range(n_tiles):
            rows = slice(i * ROW_TILE, (i + 1) * ROW_TILE)
            hsum = jnp.concatenate([h_s[0, lb, rows, :] + h_s[1, lb, rows, :] for lb in range(nlb)], axis=1)
            o_ref[rows, :] = hsum * _silu(g_ref[rows, :])
        return tuple(carries)

    zero = jnp.zeros((1, dh), F32)
    states = seg(xc_ref, gc_ref, oc_ref, xc_ref.shape[0], (zero, zero))
    seg(xx_ref, gx_ref, ox_ref, xx_ref.shape[0], states)


def _rglru(zc, zx, cw, cb, wg, br, bi, lam):
    bsz, tc_len, _ = zc.shape
    tx_len = zx.shape[1]
    dh = D_HEADS
    nb = dh // LANES

    def zspec(t_len, col):
        return pl.BlockSpec((None, t_len, dh), lambda b: (b, 0, col // nb))

    def full(a):
        return pl.BlockSpec(a.shape, lambda b: (0,) * a.ndim)

    return pl.pallas_call(
        _rglru_kernel,
        grid=(bsz,),
        in_specs=[zspec(tc_len, COL_XA), zspec(tc_len, COL_GA), zspec(tx_len, COL_XA), zspec(tx_len, COL_GA),
                  full(cw), full(cb), full(wg), full(br), full(bi), full(lam)],
        out_specs=[pl.BlockSpec((None, tc_len, dh), lambda b: (b, 0, 0)),
                   pl.BlockSpec((None, tx_len, dh), lambda b: (b, 0, 0))],
        out_shape=[jax.ShapeDtypeStruct((bsz, tc_len, dh), F32),
                   jax.ShapeDtypeStruct((bsz, tx_len, dh), F32)],
        scratch_shapes=[pltpu.VMEM((nb, tx_len + 2 * SUBLANES, LANES), F32),
                        pltpu.VMEM((2, nb, tx_len, LANES), F32),
                        pltpu.VMEM((tx_len, dh), F32)],
        compiler_params=_params(),
        name="rglru",
    )(zc, zc, zx, zx, cw, cb, wg, br, bi, lam)


GDN_UNROLL = 8
GDN_SPLICES = 4


def _gdn_masks():
    c = CHUNK_B
    i = np.arange(c)[:, None]
    k = np.arange(c)[None, :]
    stacked = np.stack([np.concatenate([k <= i, k > i], 0), np.concatenate([k >= i, k < i], 0)])
    return np.tile(stacked, (1, 1, 3)).astype(np.float32)


def _split3_rows(x):
    hi = x.astype(BF16)
    r1 = x - hi.astype(F32)
    mid = r1.astype(BF16)
    lo = (r1 - mid.astype(F32)).astype(BF16)
    return jnp.concatenate([hi, mid, lo], axis=0)


INV_BLOCK = 16


def _gdn_kernel(qc_ref, kc_ref, vc_ref, abc_ref, gtc_ref, qx_ref, kx_ref, vx_ref, abx_ref, gtx_ref,
                cwq_ref, cwk_ref, cwv_ref, prm_ref, nw_ref, mst_ref, oc_ref, ox_ref,
                pad_ref, q_s, k_s, v_s, gf_s, bf_s, gb_s, bb_s, o_s, qp_s, p_s, n_s, et_s,
                o_c, qp_c, p_c, n_c, et_c):
    head = pl.program_id(1)
    c = CHUNK_B
    ii = lax.broadcasted_iota(jnp.int32, (c, c), 0)
    jj = lax.broadcasted_iota(jnp.int32, (c, c), 1)
    eye = (ii == jj).astype(F32)
    a_mask = (ii > jj, ii < jj)
    s_mask = (ii >= jj, ii <= jj)
    d_mask = (ii // INV_BLOCK) == (jj // INV_BLOCK)
    f_mask = []
    size = INV_BLOCK
    while size < c:
        same = (ii // (2 * size)) == (jj // (2 * size))
        hi_i, hi_j = (ii // size) % 2, (jj // size) % 2
        f_mask.append((same & (hi_i == 1) & (hi_j == 0), same & (hi_i == 0) & (hi_j == 1)))
        size *= 2
    sel_r = lax.broadcasted_iota(jnp.int32, (3 * LANES, 4 * LANES), 0) % LANES
    sel_b = lax.broadcasted_iota(jnp.int32, (3 * LANES, 4 * LANES), 1) // LANES
    sel4 = (sel_r == head + H_B * (2 * (sel_b % 2) + sel_b // 2)).astype(BF16)
    lane = lax.broadcasted_iota(jnp.int32, (ROW_TILE, LANES), 1)

    def prep(q_ref, k_ref, v_ref, ab_ref, t_len):
        n_tiles = t_len // ROW_TILE
        for src, dst, cw_ref, kind in ((q_ref, q_s, cwq_ref, "q"), (k_ref, k_s, cwk_ref, "k"),
                                       (v_ref, v_s, cwv_ref, "v")):
            _fill_pad(pad_ref, src, t_len)
            for i in range(n_tiles):
                xc = _silu(_conv_tile(pad_ref, cw_ref, i))
                if kind != "v":
                    xc = xc * lax.rsqrt(jnp.sum(xc * xc, axis=-1, keepdims=True) + EPS)
                if kind == "q":
                    xc = xc * (DK_B ** -0.5)
                dst[i * ROW_TILE:(i + 1) * ROW_TILE, :] = xc
        for i in range(n_tiles):
            rows = slice(i * ROW_TILE, (i + 1) * ROW_TILE)
            ab = ab_ref[rows, :]
            gval = -jnp.exp(prm_ref[0:1, :]) * _softplus(ab + prm_ref[1:2, :])
            gbv = jnp.where(lane < 2 * H_B, gval, _sigmoid(ab))
            hi = gbv.astype(BF16)
            r1 = gbv - hi.astype(F32)
            mid = r1.astype(BF16)
            lo = (r1 - mid.astype(F32)).astype(BF16)
            bc = jnp.dot(jnp.concatenate([hi, mid, lo], axis=1), sel4, preferred_element_type=F32)
            for j, dst in enumerate((gf_s, bf_s, gb_s, bb_s)):
                dst[rows, :] = bc[:, j * LANES:(j + 1) * LANES]

    def chunk_terms(items, scr, hooks=()):
        o_s, qp_s, p_s, n_s, et_s = scr
        hooks = list(hooks)
        per_splice = -(-len(hooks) // GDN_SPLICES)

        def splice():
            for _ in range(min(per_splice, len(hooks))):
                hooks.pop(0)()

        m = range(len(items))
        dirs = [d for d, _ in items]
        rows = [pl.ds(pl.multiple_of(ci * c, c), c) for _, ci in items]
        prow = [pl.ds(pl.multiple_of(ci * LANES, LANES), LANES) for _, ci in items]
        erow = [pl.ds(pl.multiple_of(ci * SUBLANES, SUBLANES), SUBLANES) for _, ci in items]
        q = [q_s[r, :] for r in rows]
        k = [k_s[r, :] for r in rows]
        v = [v_s[r, :] for r in rows]
        g = [(gf_s, gb_s)[d][r, :] for d, r in zip(dirs, rows)]
        beta = [(bf_s, bb_s)[d][r, :] for d, r in zip(dirs, rows)]
        cum = [jnp.dot(mst_ref[dirs[i]], _split3_rows(g[i]), preferred_element_type=F32) for i in m]
        kb = [k[i] * beta[i] for i in m]
        kq = [_dot_nt(jnp.concatenate([kb[i], q[i]], axis=0), k[i]) for i in m]
        splice()
        e_in = [jnp.exp(cum[i][0:c]) for i in m]
        e_rest = [jnp.exp(cum[i][c:2 * c]) for i in m]
        cs = [cum[i][0:c, 0:c] for i in m]
        dec = [jnp.exp(jnp.where(s_mask[dirs[i]], cs[i] - cs[i].T, 0.0)) for i in m]
        for i in m:
            et_s[dirs[i], erow[i], :] = jnp.exp((cum[i][0:SUBLANES] + cum[i][c:c + SUBLANES]))
        a = [jnp.where(a_mask[dirs[i]], kq[i][:c] * dec[i], 0.0) for i in m]
        qk = [jnp.where(s_mask[dirs[i]], kq[i][c:] * dec[i], 0.0) for i in m]
        dg = [jnp.where(d_mask, a[i], 0.0) for i in m]
        x1 = [_dot(dg[i], dg[i]) for i in m]
        x2 = [_dot(x1[i], x1[i]) for i in m]
        splice()
        st = [_dot(jnp.concatenate([x1[i], x2[i]], axis=0), x2[i]) for i in m]
        q2 = [x1[i] + x2[i] + st[i][:c] for i in m]
        q3 = [q2[i] + st[i][c:] + _dot(q2[i], st[i][c:]) for i in m]
        imd = [eye - dg[i] for i in m]
        t_inv = [imd[i] + _dot(imd[i], q3[i]) for i in m]
        splice()
        for fm in f_mask:
            tf = [_dot(t_inv[i], jnp.where(fm[dirs[i]], a[i], 0.0)) for i in m]
            t_inv = [t_inv[i] - _dot(tf[i], t_inv[i]) for i in m]
        splice()
        uw = [_dot(t_inv[i], jnp.concatenate([v[i] * beta[i], kb[i] * e_in[i]], axis=1)) for i in m]
        res = [_dot(jnp.concatenate([qk[i], (k[i] * e_rest[i]).T], axis=0), uw[i]) for i in m]
        for i in m:
            n_s[dirs[i], prow[i], :] = res[i][c:, :LANES]
            p_s[dirs[i], prow[i], :] = res[i][c:, LANES:].astype(BF16)
            o_s[dirs[i], rows[i], :] = res[i][:c, :LANES]
            qp_s[dirs[i], rows[i], :] = (q[i] * e_in[i] - res[i][:c, LANES:]).astype(BF16)

    state = [(jnp.zeros((LANES, LANES), F32),) * 2]

    def seg(q_ref, k_ref, v_ref, ab_ref, gt_ref, o_ref, scr, pending):
        o_s, qp_s, p_s, n_s, et_s = scr
        t_len = q_ref.shape[0]
        n_chunks = t_len // c
        prep(q_ref, k_ref, v_ref, ab_ref, t_len)

        unroll = min(GDN_UNROLL, n_chunks)
        n_full = n_chunks // unroll

        def terms(start, size, hooks):
            chunk_terms([(dirn, start + j if dirn == 0 else n_chunks - 1 - (start + j))
                         for j in range(size) for dirn in range(2)], scr, hooks)

        def state_hooks(start, size):
            def hook(n):
                def run():
                    st, new = state[0], []
                    for dirn, ci in ((0, n), (1, n_chunks - 1 - n)):
                        rows = pl.ds(pl.multiple_of(ci * c, c), c)
                        prow = pl.ds(pl.multiple_of(ci * LANES, LANES), LANES)
                        sb = st[dirn].astype(BF16)
                        o_s[dirn, rows, :] = o_s[dirn, rows, :] + jnp.dot(qp_s[dirn, rows, :], sb,
                                                                         preferred_element_type=F32)
                        e_tot = et_s[dirn, pl.ds(pl.multiple_of(ci * SUBLANES, SUBLANES), 1), :]
                        new.append(st[dirn] * e_tot + n_s[dirn, prow, :]
                                   - jnp.dot(p_s[dirn, prow, :], sb, preferred_element_type=F32))
                    state[0] = tuple(new)
                return run
            return [hook(start + j) for j in range(size)]

        def finish():
            for i in range(t_len // ROW_TILE):
                rows = slice(i * ROW_TILE, (i + 1) * ROW_TILE)
                o = o_s[0, rows, :] + o_s[1, rows, :]
                o_ref[rows, :] = _rms(o) * nw_ref[...] * _silu(gt_ref[rows, :])

        terms(0, unroll, pending)

        def pass_body(p, st):
            state[0] = st
            terms(p * unroll, unroll, state_hooks((p - 1) * unroll, unroll))
            return state[0]

        state[0] = lax.fori_loop(1, n_full, pass_body, state[0])
        return state_hooks((n_full - 1) * unroll, unroll) + [finish]

    pending = seg(qc_ref, kc_ref, vc_ref, abc_ref, gtc_ref, oc_ref, (o_c, qp_c, p_c, n_c, et_c), [])
    for run in seg(qx_ref, kx_ref, vx_ref, abx_ref, gtx_ref, ox_ref, (o_s, qp_s, p_s, n_s, et_s), pending):
        run()


def _gdn(zc, zx, cwb, prm, nw):
    bsz, tc_len, _ = zc.shape
    tx_len = zx.shape[1]

    def zspec(t_len, col, per_head=True):
        if per_head:
            return pl.BlockSpec((None, t_len, LANES), lambda b, h: (b, 0, col + h))
        return pl.BlockSpec((None, t_len, LANES), lambda b, h: (b, 0, col))

    def seg_specs(t_len):
        return [zspec(t_len, COL_Q), zspec(t_len, COL_K), zspec(t_len, COL_V),
                zspec(t_len, COL_AB, False), zspec(t_len, COL_GB)]

    def cw_spec(off):
        return pl.BlockSpec((CONV_W, LANES), lambda b, h: (0, off + h))

    mst = jnp.asarray(_gdn_masks(), BF16)

    def terms_scratch(t_len):
        n_chunks = t_len // CHUNK_B
        return [pltpu.VMEM((2, t_len, LANES), F32),
                pltpu.VMEM((2, t_len, LANES), BF16),
                pltpu.VMEM((2, n_chunks * LANES, LANES), BF16),
                pltpu.VMEM((2, n_chunks * LANES, LANES), F32),
                pltpu.VMEM((2, n_chunks * SUBLANES, LANES), F32)]

    return pl.pallas_call(
        _gdn_kernel,
        grid=(bsz, H_B),
        in_specs=seg_specs(tc_len) + seg_specs(tx_len) + [
            cw_spec(0), cw_spec(H_B), cw_spec(2 * H_B),
            pl.BlockSpec(prm.shape, lambda b, h: (0, 0)),
            pl.BlockSpec(nw.shape, lambda b, h: (0, 0)),
            pl.BlockSpec(mst.shape, lambda b, h: (0, 0, 0))],
        out_specs=[pl.BlockSpec((None, tc_len, LANES), lambda b, h: (b, 0, h)),
                   pl.BlockSpec((None, tx_len, LANES), lambda b, h: (b, 0, h))],
        out_shape=[jax.ShapeDtypeStruct((bsz, tc_len, D_HEADS), F32),
                   jax.ShapeDtypeStruct((bsz, tx_len, D_HEADS), F32)],
        scratch_shapes=[pltpu.VMEM((tx_len + 2 * SUBLANES, LANES), F32)]
                       + [pltpu.VMEM((tx_len, LANES), F32) for _ in range(7)]
                       + terms_scratch(tx_len) + terms_scratch(tc_len),
        compiler_params=_params(),
        name="gdn",
    )(zc, zc, zc, zc, zc, zx, zx, zx, zx, zx, cwb, cwb, cwb, prm, nw, mst)


_LEVELS = (1, 2, 4, 8, 16, 32, 64)


def _gla_masks():
    n = TILE_C
    i = np.arange(n)[:, None]
    k = np.arange(n)[None, :]
    sums, scores = [], []
    for lv in _LEVELS:
        same = (i // (2 * lv)) == (k // (2 * lv))
        hi_i, hi_k = (i // lv) % 2, (k // lv) % 2
        if lv < SUBLANES:
            sums.append(same & np.where(hi_i == 1, (hi_k == 1) & (k <= i), (hi_k == 0) & (k > i)))
        scores.append(same & (hi_i == 1) & (hi_k == 0))
    sums += [k <= i, k > i]
    scores.append(i == k)
    sums_f = np.stack(sums).astype(np.float32)
    scores_f = np.stack(scores).astype(np.float32)
    sums_all = np.stack([sums_f, sums_f[:, ::-1, ::-1]]).reshape(2, len(sums) * n, n)
    scores_all = np.stack([scores_f, scores_f[:, ::-1, ::-1]])
    return np.tile(sums_all, (1, 1, 2)), scores_all


GLA_UNROLL = 4


def _gla_kernel(qc_ref, ffc_ref, fbc_ref, ic_ref, gtc_ref, qx_ref, ffx_ref, fbx_ref, ix_ref, gtx_ref,
                lb_ref, nw_ref, msum_ref, mscore_ref, oc_ref, ox_ref, oi_s, qd_s, u_s, et_s, *, layer):
    n = TILE_C
    n_lv = len(_LEVELS)
    depth = lb_ref.shape[0] // 2
    lbs = []
    for dirn in range(2):
        rows = [lb_ref[2 * j + dirn:2 * j + dirn + 1, :] for j in range(depth)]
        mx = functools.reduce(jnp.maximum, rows)
        ex = [jnp.exp(r - mx) for r in rows]
        den = functools.reduce(lambda p, q: p + q, ex)
        acc = jnp.zeros_like(mx)
        for j in range(1, layer + 1):
            acc = acc + ex[j] / den
        lbs.append(acc)

    def tile_terms(items, q_ref, f_refs, i_ref):
        m = range(len(items))
        dirs = [d for d, _ in items]
        rows = [pl.ds(pl.multiple_of(ti * n, n), n) for _, ti in items]
        q = [_silu(q_ref[r, :]) for r in rows]
        v = [i_ref[r, :] for r in rows]
        fg = [lbs[dirs[i]] + (1.0 - lbs[dirs[i]]) * _sigmoid(f_refs[dirs[i]][rows[i], :]) for i in m]
        k = [1.0 - fg[i] for i in m]
        lf = [jnp.log(fg[i]) for i in m]
        hi = [lf[i].astype(BF16) for i in m]
        lf2 = [jnp.concatenate([hi[i], (lf[i] - hi[i].astype(F32)).astype(BF16)], axis=0) for i in m]

        sums = [jnp.dot(msum_ref[dirs[i]], lf2[i], preferred_element_type=F32) for i in m]

        def csum(j):
            return [sums[i][j * n:(j + 1) * n, :] for i in m]

        n_small = sum(1 for lv in _LEVELS if lv < SUBLANES)
        c_in, c_rest = csum(n_small), csum(n_small + 1)

        def boundary_exponent(i, lv):
            off = lv - 1 if dirs[i] == 0 else lv
            ref = jnp.concatenate(
                [jnp.broadcast_to(c_in[i][blk * 2 * lv + off:blk * 2 * lv + off + 1, :], (2 * lv, LANES))
                 for blk in range(n // (2 * lv))], axis=0)
            return -jnp.abs(c_in[i] - ref)

        qb = [q[i].astype(BF16) for i in m]
        kb = [k[i].astype(BF16) for i in m]
        scores = [mscore_ref[dirs[i], n_lv] * _dot_nt(qb[i], kb[i]).astype(BF16) for i in m]
        for j, lv in enumerate(_LEVELS):
            xs = csum(j) if lv < SUBLANES else [boundary_exponent(i, lv) for i in m]
            e = [jnp.exp(x).astype(BF16) for x in xs]
            p = [_dot_nt(qb[i] * e[i], kb[i] * e[i]) for i in m]
            scores = [scores[i] + mscore_ref[dirs[i], j] * p[i].astype(BF16) for i in m]
        o_intra = [_dot(scores[i], v[i]) for i in m]
        u = [_dot_tn(v[i], k[i] * jnp.exp(c_rest[i])) for i in m]
        for i, (_, ti) in enumerate(items):
            oi_s[dirs[i], rows[i], :] = o_intra[i]
            qd_s[dirs[i], rows[i], :] = (q[i] * jnp.exp(c_in[i])).astype(BF16)
            u_s[dirs[i], rows[i], :] = u[i]
            et_s[dirs[i], pl.ds(pl.multiple_of(ti * SUBLANES, SUBLANES), SUBLANES), :] = jnp.exp(
                (c_in[i] + c_rest[i])[0:SUBLANES, :])

    def seg(q_ref, ff_ref, fb_ref, i_ref, gt_ref, o_ref, states):
        t_len = q_ref.shape[0]
        n_tiles = t_len // n

        unroll = min(GLA_UNROLL, n_tiles)
        n_pass = n_tiles // unroll

        def terms(p):
            tile_terms([(dirn, p * unroll + j if dirn == 0 else n_tiles - 1 - (p * unroll + j))
                        for j in range(unroll) for dirn in range(2)], q_ref, (ff_ref, fb_ref), i_ref)

        def state_steps(p, st):
            for j in range(unroll):
                ti = p * unroll + j
                new = []
                for dirn, tj in ((0, ti), (1, n_tiles - 1 - ti)):
                    rows = pl.ds(pl.multiple_of(tj * n, n), n)
                    oi_s[dirn, rows, :] = oi_s[dirn, rows, :] + lax.dot_general(
                        qd_s[dirn, rows, :], st[dirn].astype(BF16), (((1,), (1,)), ((), ())),
                        preferred_element_type=F32)
                    e_tot = et_s[dirn, pl.ds(pl.multiple_of(tj * SUBLANES, SUBLANES), 1), :]
                    new.append(st[dirn] * e_tot + u_s[dirn, rows, :])
                st = tuple(new)
            return st

        terms(0)

        def pass_body(p, st):
            st = state_steps(p - 1, st)
            terms(p)
            return st

        states = lax.fori_loop(1, n_pass, pass_body, states)
        states = state_steps(n_pass - 1, states)
        for i in range(t_len // ROW_TILE):
            rows = slice(i * ROW_TILE, (i + 1) * ROW_TILE)
            o = oi_s[0, rows, :] + oi_s[1, rows, :]
            o_ref[rows, :] = _rms(o) * nw_ref[...] * _silu(gt_ref[rows, :])
        return states

    zero = jnp.zeros((LANES, LANES), F32)
    states = seg(qc_ref, ffc_ref, fbc_ref, ic_ref, gtc_ref, oc_ref, (zero, zero))
    seg(qx_ref, ffx_ref, fbx_ref, ix_ref, gtx_ref, ox_ref, states)


def _gla(zc, zx, lb, nw, msum, mscore, layer):
    bsz, tc_len, _ = zc.shape
    tx_len = zx.shape[1]

    def zspec(t_len, col):
        return pl.BlockSpec((None, t_len, LANES), lambda b, h: (b, 0, col + h))

    def seg_specs(t_len):
        return [zspec(t_len, COL_QC), zspec(t_len, COL_FF), zspec(t_len, COL_FB),
                zspec(t_len, COL_IC), zspec(t_len, COL_GC)]

    return pl.pallas_call(
        functools.partial(_gla_kernel, layer=layer),
        grid=(bsz, H_C),
        in_specs=seg_specs(tc_len) + seg_specs(tx_len) + [
            pl.BlockSpec((lb.shape[0], LANES), lambda b, h: (0, h)),
            pl.BlockSpec(nw.shape, lambda b, h: (0, 0)),
            pl.BlockSpec(msum.shape, lambda b, h: (0, 0, 0)),
            pl.BlockSpec(mscore.shape, lambda b, h: (0, 0, 0, 0))],
        out_specs=[pl.BlockSpec((None, tc_len, LANES), lambda b, h: (b, 0, h)),
                   pl.BlockSpec((None, tx_len, LANES), lambda b, h: (b, 0, h))],
        out_shape=[jax.ShapeDtypeStruct((bsz, tc_len, D_HEADS), F32),
                   jax.ShapeDtypeStruct((bsz, tx_len, D_HEADS), F32)],
        scratch_shapes=[pltpu.VMEM((2, tx_len, LANES), F32),
                        pltpu.VMEM((2, tx_len, LANES), BF16),
                        pltpu.VMEM((2, tx_len, LANES), F32),
                        pltpu.VMEM((2, (tx_len // TILE_C) * SUBLANES, LANES), F32)],
        compiler_params=_params(),
        name="hgrn2",
    )(zc, zc, zc, zc, zc, zx, zx, zx, zx, zx, lb, nw, msum, mscore)


def _block_diag_halves(w):
    w4 = w.reshape(2, H_A // 2, HD_A, HD_A)
    eye = jnp.eye(H_A // 2, dtype=w.dtype)
    return jnp.einsum("ghij,hk->ghikj", w4, eye).reshape(2, (H_A // 2) * HD_A, (H_A // 2) * HD_A)


def kernel(x, c, ctx, c_ctx, w_ada, b_ada, norm_pre, norm_post, w_in, conv_a_w, conv_a_b, rg_w_r, rg_b_r,
           rg_w_i, rg_b_i, rg_lam, conv_b_w, gdn_a_log, gdn_dt_bias, gdn_norm, hg_lb, hg_norm, w_out):
    bsz, t_len, d = x.shape
    depth = w_ada.shape[0]
    dh = D_HEADS

    cs = jnp.concatenate([c, c_ctx[None, :], jnp.zeros((16 - bsz - 1, d), F32)], axis=0)
    mod = _ada(cs, w_ada, b_ada)

    qkv_end = 2 * dh + 3 * dh
    w_in_b = w_in.astype(BF16)
    w_in_p = jnp.concatenate(
        [w_in_b[:, :, :qkv_end], w_in_b[:, :, qkv_end + 4 * H_B:], w_in_b[:, :, qkv_end:qkv_end + 4 * H_B],
         jnp.zeros((depth, d, LANES - 4 * H_B), BF16)], axis=-1)
    w_out_b = w_out.astype(BF16)
    msum_np, mscore_np = _gla_masks()
    msum = jnp.asarray(msum_np, BF16)
    mscore = jnp.asarray(mscore_np, BF16)
    lb2 = hg_lb.reshape(depth * 2, dh)

    h, hc = x, ctx
    for l in range(depth):
        colmajor = l % 2 == 1
        mx = mod[l, :bsz].reshape(bsz, 1, 3 * d)
        mc = jnp.broadcast_to(mod[l, bsz].reshape(1, 1, 3 * d), (bsz, 1, 3 * d))
        npre = norm_pre[l].reshape(1, d)
        npost = norm_post[l].reshape(1, d)
        zc = _inproj(hc, mc[..., :d], mc[..., d:2 * d], npre, w_in_p[l], colmajor=False)
        zx = _inproj(h, mx[..., :d], mx[..., d:2 * d], npre, w_in_p[l], colmajor=colmajor)

        wg = jnp.concatenate([_block_diag_halves(rg_w_r[l, 0]), _block_diag_halves(rg_w_i[l, 0]),
                              _block_diag_halves(rg_w_r[l, 1]), _block_diag_halves(rg_w_i[l, 1])],
                             axis=-1).astype(BF16)
        ya_c, ya_x = _rglru(zc, zx, conv_a_w[l], conv_a_b[l].reshape(1, dh), wg,
                            rg_b_r[l], rg_b_i[l], rg_lam[l])

        prm = jnp.zeros((SUBLANES, LANES), F32)
        prm = prm.at[0, :2 * H_B].set(gdn_a_log[l].reshape(-1)).at[1, :2 * H_B].set(gdn_dt_bias[l].reshape(-1))
        yb_c, yb_x = _gdn(zc, zx, conv_b_w[l], prm, gdn_norm[l].reshape(1, LANES))

        yc_c, yc_x = _gla(zc, zx, lb2, hg_norm[l].reshape(1, LANES), msum, mscore, l)

        h = _outproj(ya_x, yb_x, yc_x, w_out_b[l], h, mx[..., 2 * d:], npost, colmajor=colmajor)
        if l < depth - 1:
            hc = _outproj(ya_c, yb_c, yc_c, w_out_b[l], hc, mc[..., 2 * d:], npost, colmajor=False)
    return h
```

```python
import functools

import numpy as np
import jax
import jax.numpy as jnp
from jax import lax
from jax.experimental import pallas as pl
from jax.experimental.pallas import tpu as pltpu

F32 = jnp.float32
BF16 = jnp.bfloat16
HI = lax.Precision.HIGHEST

GRID_W = 64
H_A, HD_A = 8, 64
H_B, DK_B = 4, 128
H_C = 4
D_HEADS = 512
CONV_W = 4
CONV_PAD = 2
RG_C = 8.0
EPS = 1e-6

LANES = 128
SUBLANES = 8
ROW_TILE = 256
PROJ_TILE = 512
RG_SCAN_STEPS = 8
CHUNK_B = 64
TILE_C = 128
VMEM_LIMIT = 56 * 1024 * 1024

COL_XA, COL_GA, COL_Q, COL_K, COL_V, COL_GB = 0, 4, 8, 12, 16, 20
COL_QC, COL_FF, COL_FB, COL_IC, COL_GC, COL_AB = 24, 28, 32, 36, 40, 44


def _dot(a, b):
    return jnp.dot(a.astype(BF16), b.astype(BF16), preferred_element_type=F32)


def _dot_nt(a, b):
    return lax.dot_general(a.astype(BF16), b.astype(BF16), (((1,), (1,)), ((), ())),
                           preferred_element_type=F32)


def _dot_tn(a, b):
    return lax.dot_general(a.astype(BF16), b.astype(BF16), (((0,), (0,)), ((), ())),
                           preferred_element_type=F32)


def _dot_hi(a, b):
    return jnp.dot(a, b, precision=HI, preferred_element_type=F32)


def _sigmoid(x):
    return 1.0 / (1.0 + jnp.exp(-x))


def _silu(x):
    return x * _sigmoid(x)


def _softplus(x):
    return jnp.maximum(x, 0.0) + jnp.log1p(jnp.exp(-jnp.abs(x)))


def _rms(x):
    return x * lax.rsqrt(jnp.mean(x * x, axis=-1, keepdims=True) + EPS)


def _params(**kw):
    return pltpu.CompilerParams(vmem_limit_bytes=VMEM_LIMIT, **kw)


def _ada_kernel(s_ref, w_ref, b_ref, o_ref):
    o_ref[...] = _dot_hi(_silu(s_ref[...]), w_ref[...]) + b_ref[...]


def _ada(cs, w_ada, b_ada):
    depth, d, n3 = w_ada.shape
    tn = 1024
    return pl.pallas_call(
        _ada_kernel,
        grid=(depth, n3 // tn),
        in_specs=[pl.BlockSpec((16, d), lambda l, j: (0, 0)),
                  pl.BlockSpec((None, d, tn), lambda l, j: (l, 0, j)),
                  pl.BlockSpec((None, 1, tn), lambda l, j: (l, 0, j))],
        out_specs=pl.BlockSpec((None, 16, tn), lambda l, j: (l, 0, j)),
        out_shape=jax.ShapeDtypeStruct((depth, 16, n3), F32),
        compiler_params=_params(),
        name="ada",
    )(cs, w_ada, b_ada.reshape(depth, 1, n3))


def _inproj_kernel(h_ref, sh_ref, sc_ref, npre_ref, w_ref, z_ref, *, colmajor, cn):
    if colmajor:
        x = jnp.concatenate([h_ref[:, j, :] for j in range(h_ref.shape[1])], axis=0)
    else:
        x = h_ref[...]
    u = (_rms(x) * npre_ref[...] * (1.0 + sc_ref[...]) + sh_ref[...]).astype(BF16)
    for j in range(z_ref.shape[-1] // cn):
        z_ref[:, j * cn:(j + 1) * cn] = jnp.dot(u, w_ref[:, j * cn:(j + 1) * cn],
                                               preferred_element_type=F32)


def _inproj(h, shift, scale, npre, w, *, colmajor):
    bsz, t_len, d = h.shape
    n = w.shape[-1]
    tc = min(PROJ_TILE, t_len)
    if colmajor:
        rows = t_len // GRID_W
        h_in = h.reshape(bsz, rows, GRID_W, d)
        h_spec = pl.BlockSpec((None, rows, tc // rows, d), lambda b, i: (b, 0, i, 0))
    else:
        h_in = h
        h_spec = pl.BlockSpec((None, tc, d), lambda b, i: (b, i, 0))
    vec = pl.BlockSpec((None, 1, d), lambda b, i: (b, 0, 0))
    return pl.pallas_call(
        functools.partial(_inproj_kernel, colmajor=colmajor, cn=5760),
        grid=(bsz, t_len // tc),
        in_specs=[h_spec, vec, vec,
                  pl.BlockSpec((1, d), lambda b, i: (0, 0)),
                  pl.BlockSpec((d, n), lambda b, i: (0, 0), pipeline_mode=pl.Buffered(1))],
        out_specs=pl.BlockSpec((None, tc, n), lambda b, i: (b, i, 0)),
        out_shape=jax.ShapeDtypeStruct((bsz, t_len, n), F32),
        compiler_params=_params(),
        name="inproj",
    )(h_in, shift, scale, npre, w)


def _outproj_kernel(ya_ref, yb_ref, yc_ref, w_ref, h_ref, gt_ref, npost_ref, o_ref, *, colmajor):
    dm = ya_ref.shape[-1]
    d = npost_ref.shape[-1]
    o = (jnp.dot(ya_ref[...].astype(BF16), w_ref[0:dm, :], preferred_element_type=F32)
         + jnp.dot(yb_ref[...].astype(BF16), w_ref[dm:2 * dm, :], preferred_element_type=F32)
         + jnp.dot(yc_ref[...].astype(BF16), w_ref[2 * dm:3 * dm, :], preferred_element_type=F32))
    upd = gt_ref[...] * (_rms(o) * npost_ref[...])
    if colmajor:
        rows = h_ref.shape[0]
        for j in range(h_ref.shape[1]):
            o_ref[:, j, :] = h_ref[:, j, :] + upd[j * rows:(j + 1) * rows, :]
    else:
        o_ref[...] = h_ref[...] + upd


def _outproj(ya, yb, yc, w, h, gate, npost, *, colmajor):
    bsz, t_len, d = h.shape
    dm = ya.shape[-1]
    tc = min(PROJ_TILE, t_len)
    if colmajor:
        rows = t_len // GRID_W
        h_in = h.reshape(bsz, rows, GRID_W, d)
        h_spec = pl.BlockSpec((None, rows, tc // rows, d), lambda b, i: (b, 0, i, 0))
    else:
        h_in = h
        h_spec = pl.BlockSpec((None, tc, d), lambda b, i: (b, i, 0))
    y_spec = pl.BlockSpec((None, tc, dm), lambda b, i: (b, i, 0))
    out = pl.pallas_call(
        functools.partial(_outproj_kernel, colmajor=colmajor),
        grid=(bsz, t_len // tc),
        in_specs=[y_spec, y_spec, y_spec,
                  pl.BlockSpec((3 * dm, d), lambda b, i: (0, 0)),
                  h_spec,
                  pl.BlockSpec((None, 1, d), lambda b, i: (b, 0, 0)),
                  pl.BlockSpec((1, d), lambda b, i: (0, 0))],
        out_specs=h_spec,
        out_shape=jax.ShapeDtypeStruct(h_in.shape, F32),
        compiler_params=_params(),
        name="outproj",
    )(ya, yb, yc, w, h_in, gate, npost)
    return out.reshape(bsz, t_len, d)


def _conv_tile(pad_ref, cw_ref, i):
    base = i * ROW_TILE + SUBLANES - CONV_PAD
    acc = cw_ref[0:1, :] * pad_ref[base:base + ROW_TILE, :]
    for j in range(1, CONV_W):
        acc = acc + cw_ref[j:j + 1, :] * pad_ref[base + j:base + j + ROW_TILE, :]
    return acc


def _fill_pad(pad_ref, x_ref, t_len):
    zeros = jnp.zeros((SUBLANES, pad_ref.shape[-1]), F32)
    pad_ref[0:SUBLANES, :] = zeros
    pad_ref[SUBLANES:SUBLANES + t_len, :] = x_ref[...]
    pad_ref[SUBLANES + t_len:2 * SUBLANES + t_len, :] = zeros


def _rglru_kernel(xc_ref, gc_ref, xx_ref, gx_ref, cw_ref, cb_ref, wg_ref, br_ref, bi_ref, lam_ref,
                  oc_ref, ox_ref, pad_ref, h_s, xc_s):
    dh = xc_ref.shape[-1]
    half = dh // 2
    nv = RG_SCAN_STEPS
    sub_rows = nv * SUBLANES
    n_sub = ROW_TILE // sub_rows
    sp = _softplus(-lam_ref[...])
    sub = lax.broadcasted_iota(jnp.int32, (SUBLANES, dh), 0)

    nlb = dh // LANES

    def conv_tile(i):
        vr = []
        for b in range(n_sub):
            base = i * ROW_TILE + b * sub_rows + SUBLANES - CONV_PAD
            vr.append([jnp.concatenate([pad_ref[lb, pl.ds(base + j, SUBLANES, stride=nv), :]
                                        for lb in range(nlb)], axis=1) for j in range(nv + CONV_W - 1)])
        xc = cb_ref[...]
        for tap in range(CONV_W):
            xc = xc + cw_ref[tap:tap + 1, :] * jnp.concatenate(
                [vr[b][tap + j] for b in range(n_sub) for j in range(nv)], axis=0)
        return xc

    def gates(xc, dirn):
        pre = [_dot(xc[:, hf * half:(hf + 1) * half], wg_ref[hf, :, dirn * dh:(dirn + 1) * dh])
               for hf in range(2)]
        r_pre = jnp.concatenate([pre[0][:, :half], pre[1][:, :half]], axis=1)
        i_pre = jnp.concatenate([pre[0][:, half:], pre[1][:, half:]], axis=1)
        r = _sigmoid(r_pre + br_ref[dirn:dirn + 1, :])
        ig = _sigmoid(i_pre + bi_ref[dirn:dirn + 1, :])
        log_a = -RG_C * r * sp[dirn:dirn + 1, :]
        a = jnp.exp(log_a)
        y = -jnp.tanh(log_a) * (a * a + 1.0)
        u = jnp.where(y > 0.0, y * lax.rsqrt(y), 0.0) * (ig * xc)
        return a, u

    def tile_scan(a, u, carry, rev):
        out = [None] * n_sub
        for b in (range(n_sub)[::-1] if rev else range(n_sub)):
            rows = slice(b * sub_rows, (b + 1) * sub_rows)
            out[b], carry = sub_scan(a[rows, :], u[rows, :], carry, rev)
        return [h for hs in out for h in hs], carry

    def sub_scan(a, u, carry, rev):
        order = list(range(nv))[::-1] if rev else list(range(nv))
        a_cum, h_loc = [None] * nv, [None] * nv
        prev = None
        for j in order:
            aj, uj = a[j * SUBLANES:(j + 1) * SUBLANES, :], u[j * SUBLANES:(j + 1) * SUBLANES, :]
            if prev is None:
                a_cum[j], h_loc[j] = aj, uj
            else:
                a_cum[j], h_loc[j] = aj * a_cum[prev], aj * h_loc[prev] + uj
            prev = j
        p, r = a_cum[prev], h_loc[prev]
        for sft in (1, 2, 4):
            keep = (sub < SUBLANES - sft) if rev else (sub >= sft)
            amt = SUBLANES - sft if rev else sft
            r = r + p * jnp.where(keep, pltpu.roll(r, amt, 0), 0.0)
            p = p * jnp.where(keep, pltpu.roll(p, amt, 0), 1.0)
        state = p * carry + r
        keep = (sub < SUBLANES - 1) if rev else (sub >= 1)
        c_in = jnp.where(keep, pltpu.roll(state, SUBLANES - 1 if rev else 1, 0), carry)
        last = 0 if rev else SUBLANES - 1
        return [h_loc[j] + a_cum[j] * c_in for j in range(nv)], state[last:last + 1, :]

    def seg(x_ref, g_ref, o_ref, t_len, h0):
        n_tiles = t_len // ROW_TILE
        zeros = jnp.zeros((SUBLANES, LANES), F32)
        for lb in range(nlb):
            pad_ref[lb, 0:SUBLANES, :] = zeros
            pad_ref[lb, SUBLANES:SUBLANES + t_len, :] = x_ref[:, lb * LANES:(lb + 1) * LANES]
            pad_ref[lb, SUBLANES + t_len:2 * SUBLANES + t_len, :] = zeros
        carries = []
        for dirn in range(2):
            def tile_body(it, carry, dirn=dirn):
                i = it if dirn == 0 else n_tiles - 1 - it
                tile_rows = pl.ds(pl.multiple_of(i * ROW_TILE, ROW_TILE), ROW_TILE)
                if dirn == 0:
                    xc = conv_tile(i)
                    xc_s[tile_rows, :] = xc
                else:
                    xc = xc_s[tile_rows, :]
                a, u = gates(xc, dirn)
                hs, carry = tile_scan(a, u, carry, dirn == 1)
                for bj in range(n_sub * nv):
                    start = i * ROW_TILE + (bj // nv) * sub_rows + bj % nv
                    for lb in range(nlb):
                        h_s[dirn, lb, pl.ds(start, SUBLANES, stride=nv), :] = (
                            hs[bj][:, lb * LANES:(lb + 1) * LANES])
                return carry

            carries.append(lax.fori_loop(0, n_tiles, tile_body, h0[dirn]))
        for i in range(n_tiles):
            rows = slice(i * ROW_TILE, (i + 1) * ROW_TILE)
            hsum = jnp.concatenate([h_s[0, lb, rows, :] + h_s[1, lb, rows, :] for lb in range(nlb)], axis=1)
            o_ref[rows, :] = hsum * _silu(g_ref[rows, :])
        return tuple(carries)

    zero = jnp.zeros((1, dh), F32)
    states = seg(xc_ref, gc_ref, oc_ref, xc_ref.shape[0], (zero, zero))
    seg(xx_ref, gx_ref, ox_ref, xx_ref.shape[0], states)


def _rglru(zc, zx, cw, cb, wg, br, bi, lam):
    bsz, tc_len, _ = zc.shape
    tx_len = zx.shape[1]
    dh = D_HEADS
    nb = dh // LANES

    def zspec(t_len, col):
        return pl.BlockSpec((None, t_len, dh), lambda b: (b, 0, col // nb))

    def full(a):
        return pl.BlockSpec(a.shape, lambda b: (0,) * a.ndim)

    return pl.pallas_call(
        _rglru_kernel,
        grid=(bsz,),
        in_specs=[zspec(tc_len, COL_XA), zspec(tc_len, COL_GA), zspec(tx_len, COL_XA), zspec(tx_len, COL_GA),
                  full(cw), full(cb), full(wg), full(br), full(bi), full(lam)],
        out_specs=[pl.BlockSpec((None, tc_len, dh), lambda b: (b, 0, 0)),
                   pl.BlockSpec((None, tx_len, dh), lambda b: (b, 0, 0))],
        out_shape=[jax.ShapeDtypeStruct((bsz, tc_len, dh), F32),
                   jax.ShapeDtypeStruct((bsz, tx_len, dh), F32)],
        scratch_shapes=[pltpu.VMEM((nb, tx_len + 2 * SUBLANES, LANES), F32),
                        pltpu.VMEM((2, nb, tx_len, LANES), F32),
                        pltpu.VMEM((tx_len, dh), F32)],
        compiler_params=_params(),
        name="rglru",
    )(zc, zc, zx, zx, cw, cb, wg, br, bi, lam)


GDN_UNROLL = 8
GDN_SPLICES = 4


def _gdn_masks():
    c = CHUNK_B
    i = np.arange(c)[:, None]
    k = np.arange(c)[None, :]
    stacked = np.stack([np.concatenate([k <= i, k > i], 0), np.concatenate([k >= i, k < i], 0)])
    return np.tile(stacked, (1, 1, 3)).astype(np.float32)


def _split3_rows(x):
    hi = x.astype(BF16)
    r1 = x - hi.astype(F32)
    mid = r1.astype(BF16)
    lo = (r1 - mid.astype(F32)).astype(BF16)
    return jnp.concatenate([hi, mid, lo], axis=0)


INV_BLOCK = 16


def _gdn_kernel(qc_ref, kc_ref, vc_ref, abc_ref, gtc_ref, qx_ref, kx_ref, vx_ref, abx_ref, gtx_ref,
                cwq_ref, cwk_ref, cwv_ref, prm_ref, nw_ref, mst_ref, oc_ref, ox_ref,
                pad_ref, q_s, k_s, v_s, gf_s, bf_s, gb_s, bb_s, o_s, qp_s, p_s, n_s, et_s,
                o_c, qp_c, p_c, n_c, et_c):
    head = pl.program_id(1)
    c = CHUNK_B
    ii = lax.broadcasted_iota(jnp.int32, (c, c), 0)
    jj = lax.broadcasted_iota(jnp.int32, (c, c), 1)
    eye = (ii == jj).astype(F32)
    a_mask = (ii > jj, ii < jj)
    s_mask = (ii >= jj, ii <= jj)
    d_mask = (ii // INV_BLOCK) == (jj // INV_BLOCK)
    f_mask = []
    size = INV_BLOCK
    while size < c:
        same = (ii // (2 * size)) == (jj // (2 * size))
        hi_i, hi_j = (ii // size) % 2, (jj // size) % 2
        f_mask.append((same & (hi_i == 1) & (hi_j == 0), same & (hi_i == 0) & (hi_j == 1)))
        size *= 2
    sel_r = lax.broadcasted_iota(jnp.int32, (3 * LANES, 4 * LANES), 0) % LANES
    sel_b = lax.broadcasted_iota(jnp.int32, (3 * LANES, 4 * LANES), 1) // LANES
    sel4 = (sel_r == head + H_B * (2 * (sel_b % 2) + sel_b // 2)).astype(BF16)
    lane = lax.broadcasted_iota(jnp.int32, (ROW_TILE, LANES), 1)

    def prep(q_ref, k_ref, v_ref, ab_ref, t_len):
        n_tiles = t_len // ROW_TILE
        for src, dst, cw_ref, kind in ((q_ref, q_s, cwq_ref, "q"), (k_ref, k_s, cwk_ref, "k"),
                                       (v_ref, v_s, cwv_ref, "v")):
            _fill_pad(pad_ref, src, t_len)
            for i in range(n_tiles):
                xc = _silu(_conv_tile(pad_ref, cw_ref, i))
                if kind != "v":
                    xc = xc * lax.rsqrt(jnp.sum(xc * xc, axis=-1, keepdims=True) + EPS)
                if kind == "q":
                    xc = xc * (DK_B ** -0.5)
                dst[i * ROW_TILE:(i + 1) * ROW_TILE, :] = xc
        for i in range(n_tiles):
            rows = slice(i * ROW_TILE, (i + 1) * ROW_TILE)
            ab = ab_ref[rows, :]
            gval = -jnp.exp(prm_ref[0:1, :]) * _softplus(ab + prm_ref[1:2, :])
            gbv = jnp.where(lane < 2 * H_B, gval, _sigmoid(ab))
            hi = gbv.astype(BF16)
            r1 = gbv - hi.astype(F32)
            mid = r1.astype(BF16)
            lo = (r1 - mid.astype(F32)).astype(BF16)
            bc = jnp.dot(jnp.concatenate([hi, mid, lo], axis=1), sel4, preferred_element_type=F32)
            for j, dst in enumerate((gf_s, bf_s, gb_s, bb_s)):
                dst[rows, :] = bc[:, j * LANES:(j + 1) * LANES]

    def chunk_terms(items, scr, hooks=()):
        o_s, qp_s, p_s, n_s, et_s = scr
        hooks = list(hooks)
        per_splice = -(-len(hooks) // GDN_SPLICES)

        def splice():
            for _ in range(min(per_splice, len(hooks))):
                hooks.pop(0)()

        m = range(len(items))
        dirs = [d for d, _ in items]
        rows = [pl.ds(pl.multiple_of(ci * c, c), c) for _, ci in items]
        prow = [pl.ds(pl.multiple_of(ci * LANES, LANES), LANES) for _, ci in items]
        erow = [pl.ds(pl.multiple_of(ci * SUBLANES, SUBLANES), SUBLANES) for _, ci in items]
        q = [q_s[r, :] for r in rows]
        k = [k_s[r, :] for r in rows]
        v = [v_s[r, :] for r in rows]
        g = [(gf_s, gb_s)[d][r, :] for d, r in zip(dirs, rows)]
        beta = [(bf_s, bb_s)[d][r, :] for d, r in zip(dirs, rows)]
        cum = [jnp.dot(mst_ref[dirs[i]], _split3_rows(g[i]), preferred_element_type=F32) for i in m]
        kb = [k[i] * beta[i] for i in m]
        kq = [_dot_nt(jnp.concatenate([kb[i], q[i]], axis=0), k[i]) for i in m]
        splice()
        e_in = [jnp.exp(cum[i][0:c]) for i in m]
        e_rest = [jnp.exp(cum[i][c:2 * c]) for i in m]
        cs = [cum[i][0:c, 0:c] for i in m]
        dec = [jnp.exp(jnp.where(s_mask[dirs[i]], cs[i] - cs[i].T, 0.0)) for i in m]
        for i in m:
            et_s[dirs[i], erow[i], :] = jnp.exp((cum[i][0:SUBLANES] + cum[i][c:c + SUBLANES]))
        a = [jnp.where(a_mask[dirs[i]], kq[i][:c] * dec[i], 0.0) for i in m]
        qk = [jnp.where(s_mask[dirs[i]], kq[i][c:] * dec[i], 0.0) for i in m]
        dg = [jnp.where(d_mask, a[i], 0.0) for i in m]
        x1 = [_dot(dg[i], dg[i]) for i in m]
        x2 = [_dot(x1[i], x1[i]) for i in m]
        splice()
        st = [_dot(jnp.concatenate([x1[i], x2[i]], axis=0), x2[i]) for i in m]
        q2 = [x1[i] + x2[i] + st[i][:c] for i in m]
        q3 = [q2[i] + st[i][c:] + _dot(q2[i], st[i][c:]) for i in m]
        imd = [eye - dg[i] for i in m]
        t_inv = [imd[i] + _dot(imd[i], q3[i]) for i in m]
        splice()
        for fm in f_mask:
            tf = [_dot(t_inv[i], jnp.where(fm[dirs[i]], a[i], 0.0)) for i in m]
            t_inv = [t_inv[i] - _dot(tf[i], t_inv[i]) for i in m]
        splice()
        uw = [_dot(t_inv[i], jnp.concatenate([v[i] * beta[i], kb[i] * e_in[i]], axis=1)) for i in m]
        res = [_dot(jnp.concatenate([qk[i], (k[i] * e_rest[i]).T], axis=0), uw[i]) for i in m]
        for i in m:
            n_s[dirs[i], prow[i], :] = res[i][c:, :LANES]
            p_s[dirs[i], prow[i], :] = res[i][c:, LANES:].astype(BF16)
            o_s[dirs[i], rows[i], :] = res[i][:c, :LANES]
            qp_s[dirs[i], rows[i], :] = (q[i] * e_in[i] - res[i][:c, LANES:]).astype(BF16)

    state = [(jnp.zeros((LANES, LANES), F32),) * 2]

    def seg(q_ref, k_ref, v_ref, ab_ref, gt_ref, o_ref, scr, pending):
        o_s, qp_s, p_s, n_s, et_s = scr
        t_len = q_ref.shape[0]
        n_chunks = t_len // c
        prep(q_ref, k_ref, v_ref, ab_ref, t_len)

        unroll = min(GDN_UNROLL, n_chunks)
        n_full = n_chunks // unroll

        def terms(start, size, hooks):
            chunk_terms([(dirn, start + j if dirn == 0 else n_chunks - 1 - (start + j))
                         for j in range(size) for dirn in range(2)], scr, hooks)

        def state_hooks(start, size):
            def hook(n):
                def run():
                    st, new = state[0], []
                    for dirn, ci in ((0, n), (1, n_chunks - 1 - n)):
                        rows = pl.ds(pl.multiple_of(ci * c, c), c)
                        prow = pl.ds(pl.multiple_of(ci * LANES, LANES), LANES)
                        sb = st[dirn].astype(BF16)
                        o_s[dirn, rows, :] = o_s[dirn, rows, :] + jnp.dot(qp_s[dirn, rows, :], sb,
                                                                         preferred_element_type=F32)
                        e_tot = et_s[dirn, pl.ds(pl.multiple_of(ci * SUBLANES, SUBLANES), 1), :]
                        new.append(st[dirn] * e_tot + n_s[dirn, prow, :]
                                   - jnp.dot(p_s[dirn, prow, :], sb, preferred_element_type=F32))
                    state[0] = tuple(new)
                return run
            return [hook(start + j) for j in range(size)]

        def finish():
            for i in range(t_len // ROW_TILE):
                rows = slice(i * ROW_TILE, (i + 1) * ROW_TILE)
                o = o_s[0, rows, :] + o_s[1, rows, :]
                o_ref[rows, :] = _rms(o) * nw_ref[...] * _silu(gt_ref[rows, :])

        terms(0, unroll, pending)

        def pass_body(p, st):
            state[0] = st
            terms(p * unroll, unroll, state_hooks((p - 1) * unroll, unroll))
            return state[0]

        state[0] = lax.fori_loop(1, n_full, pass_body, state[0])
        return state_hooks((n_full - 1) * unroll, unroll) + [finish]

    pending = seg(qc_ref, kc_ref, vc_ref, abc_ref, gtc_ref, oc_ref, (o_c, qp_c, p_c, n_c, et_c), [])
    for run in seg(qx_ref, kx_ref, vx_ref, abx_ref, gtx_ref, ox_ref, (o_s, qp_s, p_s, n_s, et_s), pending):
        run()


def _gdn(zc, zx, cwb, prm, nw):
    bsz, tc_len, _ = zc.shape
    tx_len = zx.shape[1]

    def zspec(t_len, col, per_head=True):
        if per_head:
            return pl.BlockSpec((None, t_len, LANES), lambda b, h: (b, 0, col + h))
        return pl.BlockSpec((None, t_len, LANES), lambda b, h: (b, 0, col))

    def seg_specs(t_len):
        return [zspec(t_len, COL_Q), zspec(t_len, COL_K), zspec(t_len, COL_V),
                zspec(t_len, COL_AB, False), zspec(t_len, COL_GB)]

    def cw_spec(off):
        return pl.BlockSpec((CONV_W, LANES), lambda b, h: (0, off + h))

    mst = jnp.asarray(_gdn_masks(), BF16)

    def terms_scratch(t_len):
        n_chunks = t_len // CHUNK_B
        return [pltpu.VMEM((2, t_len, LANES), F32),
                pltpu.VMEM((2, t_len, LANES), BF16),
                pltpu.VMEM((2, n_chunks * LANES, LANES), BF16),
                pltpu.VMEM((2, n_chunks * LANES, LANES), F32),
                pltpu.VMEM((2, n_chunks * SUBLANES, LANES), F32)]

    return pl.pallas_call(
        _gdn_kernel,
        grid=(bsz, H_B),
        in_specs=seg_specs(tc_len) + seg_specs(tx_len) + [
            cw_spec(0), cw_spec(H_B), cw_spec(2 * H_B),
            pl.BlockSpec(prm.shape, lambda b, h: (0, 0)),
            pl.BlockSpec(nw.shape, lambda b, h: (0, 0)),
            pl.BlockSpec(mst.shape, lambda b, h: (0, 0, 0))],
        out_specs=[pl.BlockSpec((None, tc_len, LANES), lambda b, h: (b, 0, h)),
                   pl.BlockSpec((None, tx_len, LANES), lambda b, h: (b, 0, h))],
        out_shape=[jax.ShapeDtypeStruct((bsz, tc_len, D_HEADS), F32),
                   jax.ShapeDtypeStruct((bsz, tx_len, D_HEADS), F32)],
        scratch_shapes=[pltpu.VMEM((tx_len + 2 * SUBLANES, LANES), F32)]
                       + [pltpu.VMEM((tx_len, LANES), F32) for _ in range(7)]
                       + terms_scratch(tx_len) + terms_scratch(tc_len),
        compiler_params=_params(),
        name="gdn",
    )(zc, zc, zc, zc, zc, zx, zx, zx, zx, zx, cwb, cwb, cwb, prm, nw, mst)


_LEVELS = (1, 2, 4, 8, 16, 32, 64)


def _gla_masks():
    n = TILE_C
    i = np.arange(n)[:, None]
    k = np.arange(n)[None, :]
    sums, scores = [], []
    for lv in _LEVELS:
        same = (i // (2 * lv)) == (k // (2 * lv))
        hi_i, hi_k = (i // lv) % 2, (k // lv) % 2
        if lv < SUBLANES:
            sums.append(same & np.where(hi_i == 1, (hi_k == 1) & (k <= i), (hi_k == 0) & (k > i)))
        scores.append(same & (hi_i == 1) & (hi_k == 0))
    sums += [k <= i, k > i]
    scores.append(i == k)
    sums_f = np.stack(sums).astype(np.float32)
    scores_f = np.stack(scores).astype(np.float32)
    sums_all = np.stack([sums_f, sums_f[:, ::-1, ::-1]]).reshape(2, len(sums) * n, n)
    scores_all = np.stack([scores_f, scores_f[:, ::-1, ::-1]])
    return np.tile(sums_all, (1, 1, 2)), scores_all


GLA_UNROLL = 4


def _gla_kernel(qc_ref, ffc_ref, fbc_ref, ic_ref, gtc_ref, qx_ref, ffx_ref, fbx_ref, ix_ref, gtx_ref,
                lb_ref, nw_ref, msum_ref, mscore_ref, oc_ref, ox_ref, oi_s, qd_s, u_s, et_s, *, layer):
    n = TILE_C
    n_lv = len(_LEVELS)
    depth = lb_ref.shape[0] // 2
    lbs = []
    for dirn in range(2):
        rows = [lb_ref[2 * j + dirn:2 * j + dirn + 1, :] for j in range(depth)]
        mx = functools.reduce(jnp.maximum, rows)
        ex = [jnp.exp(r - mx) for r in rows]
        den = functools.reduce(lambda p, q: p + q, ex)
        acc = jnp.zeros_like(mx)
        for j in range(1, layer + 1):
            acc = acc + ex[j] / den
        lbs.append(acc)

    def tile_terms(items, q_ref, f_refs, i_ref):
        m = range(len(items))
        dirs = [d for d, _ in items]
        rows = [pl.ds(pl.multiple_of(ti * n, n), n) for _, ti in items]
        q = [_silu(q_ref[r, :]) for r in rows]
        v = [i_ref[r, :] for r in rows]
        fg = [lbs[dirs[i]] + (1.0 - lbs[dirs[i]]) * _sigmoid(f_refs[dirs[i]][rows[i], :]) for i in m]
        k = [1.0 - fg[i] for i in m]
        lf = [jnp.log(fg[i]) for i in m]
        hi = [lf[i].astype(BF16) for i in m]
        lf2 = [jnp.concatenate([hi[i], (lf[i] - hi[i].astype(F32)).astype(BF16)], axis=0) for i in m]

        sums = [jnp.dot(msum_ref[dirs[i]], lf2[i], preferred_element_type=F32) for i in m]

        def csum(j):
            return [sums[i][j * n:(j + 1) * n, :] for i in m]

        n_small = sum(1 for lv in _LEVELS if lv < SUBLANES)
        c_in, c_rest = csum(n_small), csum(n_small + 1)

        def boundary_exponent(i, lv):
            off = lv - 1 if dirs[i] == 0 else lv
            ref = jnp.concatenate(
                [jnp.broadcast_to(c_in[i][blk * 2 * lv + off:blk * 2 * lv + off + 1, :], (2 * lv, LANES))
                 for blk in range(n // (2 * lv))], axis=0)
            return -jnp.abs(c_in[i] - ref)

        qb = [q[i].astype(BF16) for i in m]
        kb = [k[i].astype(BF16) for i in m]
        scores = [mscore_ref[dirs[i], n_lv] * _dot_nt(qb[i], kb[i]).astype(BF16) for i in m]
        for j, lv in enumerate(_LEVELS):
            xs = csum(j) if lv < SUBLANES else [boundary_exponent(i, lv) for i in m]
            e = [jnp.exp(x).astype(BF16) for x in xs]
            p = [_dot_nt(qb[i] * e[i], kb[i] * e[i]) for i in m]
            scores = [scores[i] + mscore_ref[dirs[i], j] * p[i].astype(BF16) for i in m]
        o_intra = [_dot(scores[i], v[i]) for i in m]
        u = [_dot_tn(v[i], k[i] * jnp.exp(c_rest[i])) for i in m]
        for i, (_, ti) in enumerate(items):
            oi_s[dirs[i], rows[i], :] = o_intra[i]
            qd_s[dirs[i], rows[i], :] = (q[i] * jnp.exp(c_in[i])).astype(BF16)
            u_s[dirs[i], rows[i], :] = u[i]
            et_s[dirs[i], pl.ds(pl.multiple_of(ti * SUBLANES, SUBLANES), SUBLANES), :] = jnp.exp(
                (c_in[i] + c_rest[i])[0:SUBLANES, :])

    def seg(q_ref, ff_ref, fb_ref, i_ref, gt_ref, o_ref, states):
        t_len = q_ref.shape[0]
        n_tiles = t_len // n

        unroll = min(GLA_UNROLL, n_tiles)
        n_pass = n_tiles // unroll

        def terms(p):
            tile_terms([(dirn, p * unroll + j if dirn == 0 else n_tiles - 1 - (p * unroll + j))
                        for j in range(unroll) for dirn in range(2)], q_ref, (ff_ref, fb_ref), i_ref)

        def state_steps(p, st):
            for j in range(unroll):
                ti = p * unroll + j
                new = []
                for dirn, tj in ((0, ti), (1, n_tiles - 1 - ti)):
                    rows = pl.ds(pl.multiple_of(tj * n, n), n)
                    oi_s[dirn, rows, :] = oi_s[dirn, rows, :] + lax.dot_general(
                        qd_s[dirn, rows, :], st[dirn].astype(BF16), (((1,), (1,)), ((), ())),
                        preferred_element_type=F32)
                    e_tot = et_s[dirn, pl.ds(pl.multiple_of(tj * SUBLANES, SUBLANES), 1), :]
                    new.append(st[dirn] * e_tot + u_s[dirn, rows, :])
                st = tuple(new)
            return st

        terms(0)

        def pass_body(p, st):
            st = state_steps(p - 1, st)
            terms(p)
            return st

        states = lax.fori_loop(1, n_pass, pass_body, states)
        states = state_steps(n_pass - 1, states)
        for i in range(t_len // ROW_TILE):
            rows = slice(i * ROW_TILE, (i + 1) * ROW_TILE)
            o = oi_s[0, rows, :] + oi_s[1, rows, :]
            o_ref[rows, :] = _rms(o) * nw_ref[...] * _silu(gt_ref[rows, :])
        return states

    zero = jnp.zeros((LANES, LANES), F32)
    states = seg(qc_ref, ffc_ref, fbc_ref, ic_ref, gtc_ref, oc_ref, (zero, zero))
    seg(qx_ref, ffx_ref, fbx_ref, ix_ref, gtx_ref, ox_ref, states)


def _gla(zc, zx, lb, nw, msum, mscore, layer):
    bsz, tc_len, _ = zc.shape
    tx_len = zx.shape[1]

    def zspec(t_len, col):
        return pl.BlockSpec((None, t_len, LANES), lambda b, h: (b, 0, col + h))

    def seg_specs(t_len):
        return [zspec(t_len, COL_QC), zspec(t_len, COL_FF), zspec(t_len, COL_FB),
                zspec(t_len, COL_IC), zspec(t_len, COL_GC)]

    return pl.pallas_call(
        functools.partial(_gla_kernel, layer=layer),
        grid=(bsz, H_C),
        in_specs=seg_specs(tc_len) + seg_specs(tx_len) + [
            pl.BlockSpec((lb.shape[0], LANES), lambda b, h: (0, h)),
            pl.BlockSpec(nw.shape, lambda b, h: (0, 0)),
            pl.BlockSpec(msum.shape, lambda b, h: (0, 0, 0)),
            pl.BlockSpec(mscore.shape, lambda b, h: (0, 0, 0, 0))],
        out_specs=[pl.BlockSpec((None, tc_len, LANES), lambda b, h: (b, 0, h)),
                   pl.BlockSpec((None, tx_len, LANES), lambda b, h: (b, 0, h))],
        out_shape=[jax.ShapeDtypeStruct((bsz, tc_len, D_HEADS), F32),
                   jax.ShapeDtypeStruct((bsz, tx_len, D_HEADS), F32)],
        scratch_shapes=[pltpu.VMEM((2, tx_len, LANES), F32),
                        pltpu.VMEM((2, tx_len, LANES), BF16),
                        pltpu.VMEM((2, tx_len, LANES), F32),
                        pltpu.VMEM((2, (tx_len // TILE_C) * SUBLANES, LANES), F32)],
        compiler_params=_params(),
        name="hgrn2",
    )(zc, zc, zc, zc, zc, zx, zx, zx, zx, zx, lb, nw, msum, mscore)


def _block_diag_halves(w):
    w4 = w.reshape(2, H_A // 2, HD_A, HD_A)
    eye = jnp.eye(H_A // 2, dtype=w.dtype)
    return jnp.einsum("ghij,hk->ghikj", w4, eye).reshape(2, (H_A // 2) * HD_A, (H_A // 2) * HD_A)


def kernel(x, c, ctx, c_ctx, w_ada, b_ada, norm_pre, norm_post, w_in, conv_a_w, conv_a_b, rg_w_r, rg_b_r,
           rg_w_i, rg_b_i, rg_lam, conv_b_w, gdn_a_log, gdn_dt_bias, gdn_norm, hg_lb, hg_norm, w_out):
    bsz, t_len, d = x.shape
    depth = w_ada.shape[0]
    dh = D_HEADS

    cs = jnp.concatenate([c, c_ctx[None, :], jnp.zeros((16 - bsz - 1, d), F32)], axis=0)
    mod = _ada(cs, w_ada, b_ada)

    qkv_end = 2 * dh + 3 * dh
    w_in_b = w_in.astype(BF16)
    w_in_p = jnp.concatenate(
        [w_in_b[:, :, :qkv_end], w_in_b[:, :, qkv_end + 4 * H_B:], w_in_b[:, :, qkv_end:qkv_end + 4 * H_B],
         jnp.zeros((depth, d, LANES - 4 * H_B), BF16)], axis=-1)
    w_out_b = w_out.astype(BF16)
    msum_np, mscore_np = _gla_masks()
    msum = jnp.asarray(msum_np, BF16)
    mscore = jnp.asarray(mscore_np, BF16)
    lb2 = hg_lb.reshape(depth * 2, dh)

    h, hc = x, ctx
    for l in range(depth):
        colmajor = l % 2 == 1
        mx = mod[l, :bsz].reshape(bsz, 1, 3 * d)
        mc = jnp.broadcast_to(mod[l, bsz].reshape(1, 1, 3 * d), (bsz, 1, 3 * d))
        npre = norm_pre[l].reshape(1, d)
        npost = norm_post[l].reshape(1, d)
        zc = _inproj(hc, mc[..., :d], mc[..., d:2 * d], npre, w_in_p[l], colmajor=False)
        zx = _inproj(h, mx[..., :d], mx[..., d:2 * d], npre, w_in_p[l], colmajor=colmajor)

        wg = jnp.concatenate([_block_diag_halves(rg_w_r[l, 0]), _block_diag_halves(rg_w_i[l, 0]),
                              _block_diag_halves(rg_w_r[l, 1]), _block_diag_halves(rg_w_i[l, 1])],
                             axis=-1).astype(BF16)
        ya_c, ya_x = _rglru(zc, zx, conv_a_w[l], conv_a_b[l].reshape(1, dh), wg,
                            rg_b_r[l], rg_b_i[l], rg_lam[l])

        prm = jnp.zeros((SUBLANES, LANES), F32)
        prm = prm.at[0, :2 * H_B].set(gdn_a_log[l].reshape(-1)).at[1, :2 * H_B].set(gdn_dt_bias[l].reshape(-1))
        yb_c, yb_x = _gdn(zc, zx, conv_b_w[l], prm, gdn_norm[l].reshape(1, LANES))

        yc_c, yc_x = _gla(zc, zx, lb2, hg_norm[l].reshape(1, LANES), msum, mscore, l)

        h = _outproj(ya_x, yb_x, yc_x, w_out_b[l], h, mx[..., 2 * d:], npost, colmajor=colmajor)
        if l < depth - 1:
            hc = _outproj(ya_c, yb_c, yc_c, w_out_b[l], hc, mc[..., 2 * d:], npost, colmajor=False)
    return h
```
